```python
import math
import jax, jax.numpy as jnp
from jax import lax
import numpy as np

D_MODEL = 1024
BATCH = 16
SEQ = 256
DEPTH = 2
DEC_BATCH = 2
DEC_SEQ = 2048
PAST_LEN = 512

GRID_W = 64
ROPE_BASE = 10000.0
QB = 128
CONV_W = 256
CONV_K = 3
DIFF_HEADS = 4
DIFF_DK = 32
DIFF_DV = 2 * DIFF_DK
DIFF_WIDTH = DIFF_HEADS * DIFF_DV
DIFF_QK_COLS = DIFF_HEADS * 2 * DIFF_DK
DIFF_SCALE = DIFF_DK ** -0.5
MLA_HEADS = 8
MLA_Q_RANK = 384
MLA_KV_RANK = 256
MLA_NOPE = 64
MLA_ROPE = 32
MLA_V = 64
MLA_QK = MLA_NOPE + MLA_ROPE
MLA_WIDTH = MLA_HEADS * MLA_V
MLA_SCALE = MLA_QK ** -0.5
MIX_WIDTH = CONV_W + DIFF_WIDTH + MLA_WIDTH
IN_SIZES = (CONV_W, CONV_W, CONV_W, DIFF_QK_COLS, DIFF_QK_COLS, DIFF_WIDTH, MLA_Q_RANK, MLA_KV_RANK, MLA_ROPE)
IN_COLS = 3 * CONV_W + 2 * DIFF_QK_COLS + DIFF_WIDTH + MLA_Q_RANK + MLA_KV_RANK + MLA_ROPE
D_FF = ((8 * D_MODEL // 3 + 255) // 256) * 256
DEEPNORM_ALPHA = (2 * DEPTH) ** 0.25
DEEPNORM_BETA = (8 * DEPTH) ** -0.25

kernel_name = "hybrid_diffusion_trunk_ctx_prefix_step"


def layer_norm(x, g, b, eps=1e-5):
    xf = x.astype(jnp.float32)
    mu = jnp.mean(xf, axis=-1, keepdims=True)
    var = jnp.mean(jnp.square(xf - mu), axis=-1, keepdims=True)
    return ((xf - mu) * lax.rsqrt(var + eps)).astype(x.dtype) * g + b


def rms_norm(x, w, eps=1e-6):
    xf = x.astype(jnp.float32)
    ms = jnp.mean(jnp.square(xf), axis=-1, keepdims=True)
    return (xf * lax.rsqrt(ms + eps)).astype(x.dtype) * w


def rope_1d(x, pos):
    half = x.shape[-1] // 2
    freqs = ROPE_BASE ** (-jnp.arange(half, dtype=jnp.float32) / half)
    ang = pos.astype(jnp.float32)[:, None] * freqs[None, :]
    cos = jnp.cos(ang).astype(x.dtype)
    sin = jnp.sin(ang).astype(x.dtype)
    x1, x2 = x[..., :half], x[..., half:]
    return jnp.concatenate([x1 * cos - x2 * sin, x1 * sin + x2 * cos], axis=-1)


def axial_rope(x, n_tokens):
    rows = n_tokens // GRID_W
    row = jnp.repeat(jnp.arange(rows), GRID_W)
    col = jnp.tile(jnp.arange(GRID_W), rows)
    h = x.shape[-1] // 2
    return jnp.concatenate([rope_1d(x[..., :h], row), rope_1d(x[..., h:], col)], axis=-1)


def map_query_blocks(fn, q):
    s = q.shape[-2]
    nb = s // QB
    qb = jnp.moveaxis(q.reshape(q.shape[:-2] + (nb, QB, q.shape[-1])), -3, 0)
    o = lax.map(fn, qb)
    o = jnp.moveaxis(o, 0, -3)
    return o.reshape(o.shape[:-3] + (s, o.shape[-1]))


def short_conv(u, w):
    up = jnp.pad(u, ((0, 0), (1, 1), (0, 0)))
    return up[:, :-2] * w[0] + up[:, 1:-1] * w[1] + up[:, 2:] * w[2]


def mixers(h, lp, lam, lam_init, ctx_cache):
    bsz, s, _ = h.shape
    offs = [int(o) for o in np.cumsum(IN_SIZES)[:-1]]
    proj = jnp.einsum("bsd,de->bse", h, lp["w_in"])
    a_x, a_b, a_c, d_q, d_k, d_v, m_cq, m_ckv, m_kpe = jnp.split(proj, offs, axis=-1)

    y_a = a_b * short_conv(a_c * a_x, lp["conv_w"])

    q = d_q.reshape(bsz, s, DIFF_HEADS, 2, DIFF_DK).transpose(0, 2, 3, 1, 4)
    k = d_k.reshape(bsz, s, DIFF_HEADS, 2, DIFF_DK).transpose(0, 2, 3, 1, 4)
    v = d_v.reshape(bsz, s, DIFF_HEADS, DIFF_DV).transpose(0, 2, 1, 3)

    cq = rms_norm(m_cq, lp["q_norm_w"])
    qc = jnp.einsum("bsr,re->bse", cq, lp["w_uq"]).reshape(bsz, s, MLA_HEADS, MLA_QK).transpose(0, 2, 1, 3)
    q_nope, q_pe = qc[..., :MLA_NOPE], qc[..., MLA_NOPE:]
    ckv = rms_norm(m_ckv, lp["kv_norm_w"])
    kpe = m_kpe

    if ctx_cache is None:
        new_ctx = (k, v, ckv, kpe)
        k_all, v_all, ckv_all, kpe_all = k, v, ckv, kpe
    else:
        q = axial_rope(q, s)
        k = axial_rope(k, s)
        q_pe = axial_rope(q_pe, s)
        kpe = axial_rope(kpe, s)
        ck, cv, cckv, ckpe = ctx_cache
        k_all = jnp.concatenate([k, ck], axis=3)
        v_all = jnp.concatenate([v, cv], axis=2)
        ckv_all = jnp.concatenate([ckv, cckv], axis=1)
        kpe_all = jnp.concatenate([kpe, ckpe], axis=1)
        new_ctx = None
    n_keys = ckv_all.shape[1]

    def diff_block(qb):
        sc = jnp.einsum("bhmqd,bhmkd->bhmqk", qb, k_all).astype(jnp.float32) * DIFF_SCALE
        p = jax.nn.softmax(sc, axis=-1)
        a = p[:, :, 0] - lam * p[:, :, 1]
        return jnp.einsum("bhqk,bhkd->bhqd", a.astype(v_all.dtype), v_all)

    o_b = map_query_blocks(diff_block, q)
    o_b = rms_norm(o_b, lp["diff_norm_w"]) * (1.0 - lam_init)
    y_b = o_b.transpose(0, 2, 1, 3).reshape(bsz, s, DIFF_WIDTH)

    kv = jnp.einsum("bnr,re->bne", ckv_all, lp["w_ukv"]).reshape(bsz, n_keys, MLA_HEADS, MLA_NOPE + MLA_V).transpose(0, 2, 1, 3)
    k_m = jnp.concatenate([kv[..., :MLA_NOPE], jnp.broadcast_to(kpe_all[:, None], (bsz, MLA_HEADS, n_keys, MLA_ROPE))], axis=-1)
    v_m = kv[..., MLA_NOPE:]
    q_m = jnp.concatenate([q_nope, q_pe], axis=-1)

    def mla_block(qb):
        sc = jnp.einsum("bhqd,bhkd->bhqk", qb, k_m).astype(jnp.float32) * MLA_SCALE
        p = jax.nn.softmax(sc, axis=-1)
        return jnp.einsum("bhqk,bhkd->bhqd", p.astype(v_m.dtype), v_m)

    o_c = map_query_blocks(mla_block, q_m)
    y_c = o_c.transpose(0, 2, 1, 3).reshape(bsz, s, MLA_WIDTH)

    y = jnp.einsum("bse,ed->bsd", jnp.concatenate([y_a, y_b, y_c], axis=-1), lp["w_out"])
    return y, new_ctx


def layer(x, cond, lp, layer_idx, ctx_cache):
    mod = jax.nn.silu(cond) @ lp["w_ada"] + lp["b_ada"]
    mod = mod.reshape((-1, 1, mod.shape[-1]))
    sh1, sc1, g1, sh2, sc2, g2 = jnp.split(mod, 6, axis=-1)
    lam_init = 0.8 - 0.6 * math.exp(-0.3 * layer_idx)
    lam = (jnp.exp(jnp.sum(lp["lam_q1"].astype(jnp.float32) * lp["lam_k1"].astype(jnp.float32)))
           - jnp.exp(jnp.sum(lp["lam_q2"].astype(jnp.float32) * lp["lam_k2"].astype(jnp.float32)))
           + lam_init)
    y, new_ctx = mixers(x * (1.0 + sc1) + sh1, lp, lam, lam_init, ctx_cache)
    x = layer_norm(DEEPNORM_ALPHA * x + g1 * y, lp["ln1_g"], lp["ln1_b"])
    hf = x * (1.0 + sc2) + sh2
    f = (jax.nn.silu(hf @ lp["w_ff1"]) * (hf @ lp["w_ff3"])) @ lp["w_ff2"]
    x = layer_norm(DEEPNORM_ALPHA * x + g2 * f, lp["ln2_g"], lp["ln2_b"])
    return x, new_ctx


def setup_inputs(seed: int = 0) -> dict:
    key = jax.random.key(seed)
    ks = jax.random.split(key, 32)
    f32 = jnp.float32
    L = DEPTH
    D = D_MODEL

    def nrm(k, shape, s):
        return jax.random.normal(k, shape, f32) * s

    return {
        "x_prompt": nrm(ks[0], (BATCH, SEQ, D), 1.0),
        "x_sample": nrm(ks[1], (DEC_BATCH, DEC_SEQ, D), 1.0),
        "cache_diff_k": nrm(ks[2], (DEC_BATCH, L, DIFF_HEADS, 2, PAST_LEN, DIFF_DK), 1.0),
        "cache_diff_v": nrm(ks[3], (DEC_BATCH, L, DIFF_HEADS, PAST_LEN, DIFF_DV), 1.0),
        "cache_mla_ckv": nrm(ks[4], (DEC_BATCH, L, PAST_LEN, MLA_KV_RANK), 1.0),
        "cache_mla_kpe": nrm(ks[5], (DEC_BATCH, L, PAST_LEN, MLA_ROPE), 1.0),
        "c": nrm(ks[6], (DEC_BATCH, D), 1.0),
        "c_ctx": nrm(ks[7], (D,), 1.0),
        "w_ada": nrm(ks[8], (L, D, 6 * D), D ** -0.5),
        "b_ada": nrm(ks[9], (L, 6 * D), 0.02),
        "w_in": nrm(ks[10], (L, D, IN_COLS), D ** -0.5),
        "conv_w": nrm(ks[11], (L, CONV_K, CONV_W), CONV_K ** -0.5),
        "lam_q1": nrm(ks[12], (L, DIFF_DK), 0.1),
        "lam_k1": nrm(ks[13], (L, DIFF_DK), 0.1),
        "lam_q2": nrm(ks[14], (L, DIFF_DK), 0.1),
        "lam_k2": nrm(ks[15], (L, DIFF_DK), 0.1),
        "diff_norm_w": 1.0 + nrm(ks[16], (L, DIFF_DV), 0.02),
        "q_norm_w": 1.0 + nrm(ks[17], (L, MLA_Q_RANK), 0.02),
        "w_uq": nrm(ks[18], (L, MLA_Q_RANK, MLA_HEADS * MLA_QK), MLA_Q_RANK ** -0.5),
        "kv_norm_w": 1.0 + nrm(ks[19], (L, MLA_KV_RANK), 0.02),
        "w_ukv": nrm(ks[20], (L, MLA_KV_RANK, MLA_HEADS * (MLA_NOPE + MLA_V)), MLA_KV_RANK ** -0.5),
        "w_out": nrm(ks[21], (L, MIX_WIDTH, D), MIX_WIDTH ** -0.5 * DEEPNORM_BETA),
        "ln1_g": 1.0 + nrm(ks[22], (L, D), 0.02),
        "ln1_b": nrm(ks[23], (L, D), 0.02),
        "w_ff1": nrm(ks[24], (L, D, D_FF), D ** -0.5),
        "w_ff3": nrm(ks[25], (L, D, D_FF), D ** -0.5),
        "w_ff2": nrm(ks[26], (L, D_FF, D), D_FF ** -0.5 * DEEPNORM_BETA),
        "ln2_g": 1.0 + nrm(ks[27], (L, D), 0.02),
        "ln2_b": nrm(ks[28], (L, D), 0.02),
    }


def reference(x_prompt, x_sample, cache_diff_k, cache_diff_v, cache_mla_ckv, cache_mla_kpe, c, c_ctx,
              w_ada, b_ada, w_in, conv_w, lam_q1, lam_k1, lam_q2, lam_k2, diff_norm_w, q_norm_w, w_uq,
              kv_norm_w, w_ukv, w_out, ln1_g, ln1_b, w_ff1, w_ff3, w_ff2, ln2_g, ln2_b):
    xp = x_prompt
    xs = x_sample
    st_k, st_v, st_ckv, st_kpe = [], [], [], []
    for l in range(DEPTH):
        lp = {
            "w_ada": w_ada[l], "b_ada": b_ada[l], "w_in": w_in[l], "conv_w": conv_w[l],
            "lam_q1": lam_q1[l], "lam_k1": lam_k1[l], "lam_q2": lam_q2[l], "lam_k2": lam_k2[l],
            "diff_norm_w": diff_norm_w[l], "q_norm_w": q_norm_w[l], "w_uq": w_uq[l],
            "kv_norm_w": kv_norm_w[l], "w_ukv": w_ukv[l], "w_out": w_out[l],
            "ln1_g": ln1_g[l], "ln1_b": ln1_b[l], "w_ff1": w_ff1[l], "w_ff3": w_ff3[l],
            "w_ff2": w_ff2[l], "ln2_g": ln2_g[l], "ln2_b": ln2_b[l],
        }
        xp, ctx = layer(xp, c_ctx, lp, l, None)
        st_k.append(ctx[0])
        st_v.append(ctx[1])
        st_ckv.append(ctx[2])
        st_kpe.append(ctx[3])
        cache_l = (cache_diff_k[:, l], cache_diff_v[:, l], cache_mla_ckv[:, l], cache_mla_kpe[:, l])
        xs, _ = layer(xs, c, lp, l, cache_l)
    state_diff_k = jnp.stack(st_k, axis=1)
    state_diff_v = jnp.stack(st_v, axis=1)
    state_mla_ckv = jnp.stack(st_ckv, axis=1)
    state_mla_kpe = jnp.stack(st_kpe, axis=1)
    return (xp, xs, state_diff_k, state_diff_v, state_mla_ckv, state_mla_kpe)
```

```python
import functools
import math

import numpy as np
import jax
import jax.numpy as jnp
from jax import lax
from jax.experimental import pallas as pl
from jax.experimental.pallas import tpu as pltpu

F32 = jnp.float32
BF16 = jnp.bfloat16

D_MODEL = 1024
DEPTH = 2
GRID_W = 64
ROPE_BASE = 10000.0
CONV_W = 256
DIFF_HEADS = 4
DIFF_DK = 32
DIFF_DV = 64
DIFF_WIDTH = DIFF_HEADS * DIFF_DV
DIFF_QK_COLS = DIFF_HEADS * 2 * DIFF_DK
DIFF_SCALE = DIFF_DK ** -0.5
MLA_HEADS = 8
MLA_Q_RANK = 384
MLA_KV_RANK = 256
MLA_NOPE = 64
MLA_ROPE = 32
MLA_V = 64
MLA_QK = MLA_NOPE + MLA_ROPE
MLA_WIDTH = MLA_HEADS * MLA_V
MLA_SCALE = MLA_QK ** -0.5
IN_COLS = 3 * CONV_W + 2 * DIFF_QK_COLS + DIFF_WIDTH + MLA_Q_RANK + MLA_KV_RANK + MLA_ROPE
D_FF = 2816
DEEPNORM_ALPHA = (2 * DEPTH) ** 0.25
LOG2E = 1.4426950408889634

LANES = 128
IN_COLS_PAD = 2304
MLA_HEAD_PAD = LANES
MLA_QK_PAD = MLA_HEADS * MLA_HEAD_PAD
N_MOD = 6
COND_ROWS = 8
FF_CHUNK = 1408
VMEM_LIMIT = 56 * 2 ** 20

OFF_AX, OFF_AB, OFF_AC = 0, 256, 512
OFF_DQ, OFF_DK, OFF_DV = 768, 1024, 1280
OFF_CQ, OFF_CKV, OFF_KPE = 1536, 1920, 2176


def _cparams(n_grid):
    return pltpu.CompilerParams(dimension_semantics=("arbitrary",) * n_grid,
                                vmem_limit_bytes=VMEM_LIMIT)


def _layer_spec(tail, l):
    nd = len(tail)
    return pl.BlockSpec((None,) + tuple(tail), lambda *_: (l,) + (0,) * nd,
                        pipeline_mode=pl.Buffered(1))


def _sigmoid(x):
    return 1.0 / (1.0 + jnp.exp(-x))


def _layer_norm(x, g, b):
    mu = jnp.mean(x, axis=-1, keepdims=True)
    xc = x - mu
    var = jnp.mean(xc * xc, axis=-1, keepdims=True)
    return xc * lax.rsqrt(var + 1e-5) * g + b


def _rms_norm(x, w):
    ms = jnp.mean(x * x, axis=-1, keepdims=True)
    return x * lax.rsqrt(ms + 1e-6) * w


def _rope_slab(x, cos, sin_signed):
    lane = lax.broadcasted_iota(jnp.int32, x.shape, 1)
    lo = (lane % 16) < 8
    partner = jnp.where(lo, pltpu.roll(x, LANES - 8, axis=1), pltpu.roll(x, 8, axis=1))
    return x * cos + partner * sin_signed


def _ada_kernel(cond_ref, w_ref, b_ref, o_ref):
    c = cond_ref[...]
    s = (c * _sigmoid(c)).astype(BF16)
    o_ref[0] = jnp.dot(s, w_ref[0].astype(BF16), preferred_element_type=F32) + b_ref[0]


def _ada_call(cond, w_ada, b_ada):
    n_l = w_ada.shape[0]
    tn = D_MODEL
    return pl.pallas_call(
        _ada_kernel,
        out_shape=jax.ShapeDtypeStruct((n_l, COND_ROWS, N_MOD * D_MODEL), F32),
        grid=(n_l, N_MOD * D_MODEL // tn),
        in_specs=[
            pl.BlockSpec((COND_ROWS, D_MODEL), lambda l, j: (0, 0)),
            pl.BlockSpec((1, D_MODEL, tn), lambda l, j: (l, 0, j)),
            pl.BlockSpec((1, 1, tn), lambda l, j: (l, 0, j)),
        ],
        out_specs=pl.BlockSpec((1, COND_ROWS, tn), lambda l, j: (l, 0, j)),
        compiler_params=_cparams(2),
        name="ada_mod",
    )(cond, w_ada, b_ada.reshape(n_l, 1, N_MOD * D_MODEL))


def _expand_kv(ckv, kpe_slab, w_kv_ref):
    kv = jnp.dot(ckv.astype(BF16), w_kv_ref[...], preferred_element_type=F32)
    kpe_shift = pltpu.roll(kpe_slab, MLA_NOPE, axis=1)
    k_heads = [kv[:, h * MLA_HEAD_PAD:(h + 1) * MLA_HEAD_PAD] + kpe_shift for h in range(MLA_HEADS)]
    return k_heads, kv[:, MLA_QK_PAD:MLA_QK_PAD + MLA_WIDTH]


def _cache_kv_kernel(ckv_ref, kpe_ref, w_kv_ref, kct_ref, vm_ref):
    k_heads, v_m = _expand_kv(ckv_ref[...], kpe_ref[...], w_kv_ref)
    for h in range(MLA_HEADS):
        kct_ref[h * MLA_HEAD_PAD:(h + 1) * MLA_HEAD_PAD, :] = k_heads[h].T.astype(BF16)
    vm_ref[...] = v_m.astype(BF16)


def _cache_kv_call(cache_ckv, cache_kpe_slab, w_kv):
    n_b, n_l, past, _ = cache_ckv.shape
    return pl.pallas_call(
        _cache_kv_kernel,
        out_shape=(jax.ShapeDtypeStruct((n_b, n_l, MLA_QK_PAD, past), BF16),
                   jax.ShapeDtypeStruct((n_b, n_l, past, MLA_WIDTH), BF16)),
        grid=(n_b, n_l),
        in_specs=[
            pl.BlockSpec((None, None, past, MLA_KV_RANK), lambda b, l: (b, l, 0, 0)),
            pl.BlockSpec((None, None, past, LANES), lambda b, l: (b, l, 0, 0)),
            pl.BlockSpec((None, MLA_KV_RANK, MLA_QK_PAD + MLA_WIDTH), lambda b, l: (l, 0, 0)),
        ],
        out_specs=(pl.BlockSpec((None, None, MLA_QK_PAD, past), lambda b, l: (b, l, 0, 0)),
                   pl.BlockSpec((None, None, past, MLA_WIDTH), lambda b, l: (b, l, 0, 0))),
        compiler_params=_cparams(2),
        name="cache_kv",
    )(cache_ckv, cache_kpe_slab, w_kv)


def _inproj_kernel(*refs, rope, n_seq, seq, with_state):
    it = iter(refs)
    x_ref, sc_ref, sh_ref = next(it), next(it), next(it)
    w_in_ref, qnw_ref, w_uq_ref, kvnw_ref, w_kv_ref = next(it), next(it), next(it), next(it), next(it)
    if rope:
        cd_ref, sd_ref, cm_ref, sm_ref, cp_ref, sp_ref = (next(it) for _ in range(6))
    u_ref, ab_ref, qd_ref, kt_ref, vd_ref, qm_ref, kct_ref, vm_ref = (next(it) for _ in range(8))
    if with_state:
        stk_ref, stv_ref, stc_ref, stp_ref = (next(it) for _ in range(4))

    h = (x_ref[...] * (1.0 + sc_ref[0]) + sh_ref[0]).astype(BF16)
    proj = jnp.dot(h, w_in_ref[...], preferred_element_type=F32)

    u_ref[...] = proj[:, OFF_AC:OFF_AC + CONV_W] * proj[:, OFF_AX:OFF_AX + CONV_W]
    ab_ref[...] = proj[:, OFF_AB:OFF_AB + CONV_W]

    q = proj[:, OFF_DQ:OFF_DQ + DIFF_QK_COLS]
    k = proj[:, OFF_DK:OFF_DK + DIFF_QK_COLS]
    v = proj[:, OFF_DV:OFF_DV + DIFF_WIDTH]
    if rope:
        cd, sd = cd_ref[...], sd_ref[...]
        q = jnp.concatenate([_rope_slab(q[:, j * LANES:(j + 1) * LANES], cd, sd) for j in range(2)], axis=1)
        k = jnp.concatenate([_rope_slab(k[:, j * LANES:(j + 1) * LANES], cd, sd) for j in range(2)], axis=1)
    qd_ref[...] = q.astype(BF16)
    vd_ref[...] = v.astype(BF16)
    k_t = k.T.astype(BF16)

    cq = _rms_norm(proj[:, OFF_CQ:OFF_CQ + MLA_Q_RANK], qnw_ref[0])
    qc = jnp.dot(cq.astype(BF16), w_uq_ref[...], preferred_element_type=F32)
    if rope:
        cm, sm = cm_ref[...], sm_ref[...]
        for hd in range(MLA_HEADS):
            sl = slice(hd * MLA_HEAD_PAD, (hd + 1) * MLA_HEAD_PAD)
            qm_ref[:, sl] = _rope_slab(qc[:, sl], cm, sm).astype(BF16)
    else:
        qm_ref[...] = qc.astype(BF16)

    ckv = _rms_norm(proj[:, OFF_CKV:OFF_CKV + MLA_KV_RANK], kvnw_ref[0])
    kpe_slab = proj[:, OFF_KPE:OFF_KPE + LANES]
    if rope:
        kpe_slab = _rope_slab(kpe_slab, cp_ref[...], sp_ref[...])
    k_heads, v_m = _expand_kv(ckv, kpe_slab, w_kv_ref)
    vm_ref[...] = v_m.astype(BF16)

    for j in range(max(n_seq, 1)):
        rows = slice(j * seq, (j + 1) * seq) if n_seq else slice(None)
        dst_kt = kt_ref.at[j] if n_seq else kt_ref
        dst_kct = kct_ref.at[j] if n_seq else kct_ref
        dst_kt[...] = k_t[:, rows]
        for hd in range(MLA_HEADS):
            dst_kct[hd * MLA_HEAD_PAD:(hd + 1) * MLA_HEAD_PAD, :] = k_heads[hd][rows, :].T.astype(BF16)

    if with_state:
        stc_ref[...] = ckv
        stp_ref[...] = kpe_slab[:, :MLA_ROPE]
        for j in range(n_seq):
            rows = slice(j * seq, (j + 1) * seq)
            for hm in range(2 * DIFF_HEADS):
                stk_ref[j, hm // 2, hm % 2] = k[rows, hm * DIFF_DK:(hm + 1) * DIFF_DK]
            for hd in range(DIFF_HEADS):
                stv_ref[j, hd] = v[rows, hd * DIFF_DV:(hd + 1) * DIFF_DV]


def _inproj_call(x2d, mod3, wts, l, *, n_batch, seq, tm, mod_row, rope_tabs, with_state):
    rows = n_batch * seq
    n_tiles = rows // tm
    rope = rope_tabs is not None
    n_seq = tm // seq
    t_per_seq = max(seq // tm, 1)

    def mod_spec(j):
        return pl.BlockSpec((1, 1, D_MODEL), lambda i: ((l * COND_ROWS + mod_row(i)) * N_MOD + j, 0, 0))

    def row_spec(width):
        return pl.BlockSpec((tm, width), lambda i: (i, 0))

    in_specs = [
        row_spec(D_MODEL), mod_spec(1), mod_spec(0),
        _layer_spec((D_MODEL, IN_COLS_PAD), l),
        _layer_spec((1, MLA_Q_RANK), l),
        _layer_spec((MLA_Q_RANK, MLA_QK_PAD), l),
        _layer_spec((1, MLA_KV_RANK), l),
        _layer_spec((MLA_KV_RANK, MLA_QK_PAD + MLA_WIDTH), l),
    ]
    args = [x2d, mod3, mod3, wts["w_in"], wts["q_norm_w"], wts["w_uq"], wts["kv_norm_w"], wts["w_kv"]]
    if rope:
        in_specs += [pl.BlockSpec((tm, LANES), lambda i: (i % t_per_seq, 0))] * 6
        args += list(rope_tabs)

    if n_seq:
        kt_spec = pl.BlockSpec((n_seq, DIFF_QK_COLS, seq), lambda i: (i, 0, 0))
        kct_spec = pl.BlockSpec((n_seq, MLA_QK_PAD, seq), lambda i: (i, 0, 0))
    else:
        kt_spec = pl.BlockSpec((None, DIFF_QK_COLS, tm), lambda i: (i // t_per_seq, 0, i % t_per_seq))
        kct_spec = pl.BlockSpec((None, MLA_QK_PAD, tm), lambda i: (i // t_per_seq, 0, i % t_per_seq))

    out_shape = [
        jax.ShapeDtypeStruct((rows, CONV_W), F32),
        jax.ShapeDtypeStruct((rows, CONV_W), F32),
        jax.ShapeDtypeStruct((rows, DIFF_QK_COLS), BF16),
        jax.ShapeDtypeStruct((n_batch, DIFF_QK_COLS, seq), BF16),
        jax.ShapeDtypeStruct((rows, DIFF_WIDTH), BF16),
        jax.ShapeDtypeStruct((rows, MLA_QK_PAD), BF16),
        jax.ShapeDtypeStruct((n_batch, MLA_QK_PAD, seq), BF16),
        jax.ShapeDtypeStruct((rows, MLA_WIDTH), BF16),
    ]
    out_specs = [row_spec(CONV_W), row_spec(CONV_W), row_spec(DIFF_QK_COLS), kt_spec,
                 row_spec(DIFF_WIDTH), row_spec(MLA_QK_PAD), kct_spec, row_spec(MLA_WIDTH)]
    if with_state:
        assert tm % seq == 0
        out_shape += [
            jax.ShapeDtypeStruct((n_batch, DIFF_HEADS, 2, seq, DIFF_DK), F32),
            jax.ShapeDtypeStruct((n_batch, DIFF_HEADS, seq, DIFF_DV), F32),
            jax.ShapeDtypeStruct((rows, MLA_KV_RANK), F32),
            jax.ShapeDtypeStruct((rows, MLA_ROPE), F32),
        ]
        out_specs += [
            pl.BlockSpec((n_seq, DIFF_HEADS, 2, seq, DIFF_DK), lambda i: (i, 0, 0, 0, 0)),
            pl.BlockSpec((n_seq, DIFF_HEADS, seq, DIFF_DV), lambda i: (i, 0, 0, 0)),
            row_spec(MLA_KV_RANK), row_spec(MLA_ROPE),
        ]

    kern = functools.partial(_inproj_kernel, rope=rope, n_seq=n_seq, seq=seq, with_state=with_state)
    return pl.pallas_call(
        kern, out_shape=tuple(out_shape), grid=(n_tiles,),
        in_specs=in_specs, out_specs=tuple(out_specs),
        compiler_params=_cparams(1), name=f"inproj_l{l}_{'s' if rope else 'p'}",
    )(*args)


def _softmax_parts(q, kt_refs, rows, scale_log2e):
    scores = [jnp.dot(q, kt[rows, :], preferred_element_type=F32) for kt in kt_refs]
    mx = scores[0].max(axis=-1, keepdims=True)
    for s in scores[1:]:
        mx = jnp.maximum(mx, s.max(axis=-1, keepdims=True))
    ps = [jnp.exp2((s - mx) * scale_log2e) for s in scores]
    denom = ps[0].sum(axis=-1, keepdims=True)
    for p in ps[1:]:
        denom = denom + p.sum(axis=-1, keepdims=True)
    return ps, denom


def _attn_kernel(*refs, lam_init, has_cache):
    it = iter(refs)
    lam_ref, dnw_ref = next(it), next(it)
    qd_ref, kt_ref, vd_ref, qm_ref, kct_ref, vm_ref = (next(it) for _ in range(6))
    kts, vds, kcts, vms = [kt_ref], [vd_ref], [kct_ref], [vm_ref]
    if has_cache:
        kts.append(next(it)); vds.append(next(it)); kcts.append(next(it)); vms.append(next(it))
    yb_ref, yc_ref = next(it), next(it)

    lv = lam_ref[0]
    lam = (jnp.exp(jnp.sum(lv[0:1] * lv[1:2], axis=-1, keepdims=True))
           - jnp.exp(jnp.sum(lv[2:3] * lv[3:4], axis=-1, keepdims=True)) + lam_init)

    tq = qd_ref.shape[0]
    lane = lax.broadcasted_iota(jnp.int32, (1, DIFF_WIDTH), 1)

    y_b = jnp.zeros((tq, DIFF_WIDTH), F32)
    for hd in range(DIFF_HEADS):
        parts = []
        for m in range(2):
            hm = 2 * hd + m
            rows = slice(hm * DIFF_DK, (hm + 1) * DIFF_DK)
            parts.append(_softmax_parts(qd_ref[:, rows], kts, rows, DIFF_SCALE * LOG2E))
        (p1, l1), (p2, l2) = parts
        r1 = 1.0 / l1
        r2 = lam / l2
        o = None
        for pa, pb, v_ref in zip(p1, p2, vds):
            a = (pa * r1 - pb * r2).astype(BF16)
            t = jnp.dot(a, v_ref[...], preferred_element_type=F32)
            o = t if o is None else o + t
        o = jnp.where((lane >= hd * DIFF_DV) & (lane < (hd + 1) * DIFF_DV), o, 0.0)
        ms = jnp.sum(o * o, axis=-1, keepdims=True) * (1.0 / DIFF_DV)
        y_b = y_b + o * lax.rsqrt(ms + 1e-6)
    yb_ref[...] = (y_b * dnw_ref[0] * (1.0 - lam_init)).astype(BF16)

    heads_per_half = DIFF_WIDTH // MLA_V
    for half in range(MLA_HEADS // heads_per_half):
        cols = slice(half * DIFF_WIDTH, (half + 1) * DIFF_WIDTH)
        y_c = jnp.zeros((tq, DIFF_WIDTH), F32)
        for hh in range(heads_per_half):
            hd = half * heads_per_half + hh
            rows = slice(hd * MLA_HEAD_PAD, (hd + 1) * MLA_HEAD_PAD)
            ps, denom = _softmax_parts(qm_ref[:, rows], kcts, rows, MLA_SCALE * LOG2E)
            o = None
            for p, v_ref in zip(ps, vms):
                t = jnp.dot(p.astype(BF16), v_ref[:, cols], preferred_element_type=F32)
                o = t if o is None else o + t
            o = o * (1.0 / denom)
            y_c = y_c + jnp.where((lane >= hh * MLA_V) & (lane < (hh + 1) * MLA_V), o, 0.0)
        yc_ref[:, cols] = y_c.astype(BF16)


def _attn_call(lamv, dnw, qd, kt, vd, qm, kct, vm, cache, l, *, n_batch, seq, tq, lam_init):
    t_per_seq = seq // tq
    has_cache = cache is not None

    def q_spec(width):
        return pl.BlockSpec((tq, width), lambda b, t: (b * t_per_seq + t, 0))

    in_specs = [
        pl.BlockSpec((1, 4, DIFF_DK), lambda b, t: (l, 0, 0)),
        pl.BlockSpec((1, 1, DIFF_WIDTH), lambda b, t: (l, 0, 0)),
        q_spec(DIFF_QK_COLS),
        pl.BlockSpec((None, DIFF_QK_COLS, seq), lambda b, t: (b, 0, 0)),
        pl.BlockSpec((seq, DIFF_WIDTH), lambda b, t: (b, 0)),
        q_spec(MLA_QK_PAD),
        pl.BlockSpec((None, MLA_QK_PAD, seq), lambda b, t: (b, 0, 0)),
        pl.BlockSpec((seq, MLA_WIDTH), lambda b, t: (b, 0)),
    ]
    args = [lamv, dnw, qd, kt, vd, qm, kct, vm]
    if has_cache:
        past = cache[0].shape[-1]
        in_specs += [
            pl.BlockSpec((None, None, DIFF_QK_COLS, past), lambda b, t: (b, l, 0, 0)),
            pl.BlockSpec((None, None, past, DIFF_WIDTH), lambda b, t: (b, l, 0, 0)),
            pl.BlockSpec((None, None, MLA_QK_PAD, past), lambda b, t: (b, l, 0, 0)),
            pl.BlockSpec((None, None, past, MLA_WIDTH), lambda b, t: (b, l, 0, 0)),
        ]
        args += list(cache)
    rows = n_batch * seq
    return pl.pallas_call(
        functools.partial(_attn_kernel, lam_init=lam_init, has_cache=has_cache),
        out_shape=(jax.ShapeDtypeStruct((rows, DIFF_WIDTH), BF16),
                   jax.ShapeDtypeStruct((rows, MLA_WIDTH), BF16)),
        grid=(n_batch, t_per_seq),
        in_specs=in_specs,
        out_specs=(q_spec(DIFF_WIDTH), q_spec(MLA_WIDTH)),
        compiler_params=_cparams(2), name=f"attn_l{l}_{'s' if has_cache else 'p'}",
    )(*args)


def _tail_kernel(x_ref, u_ref, up_ref, un_ref, ab_ref, yb_ref, yc_ref, cw_ref, g1_ref, sh2_ref, sc2_ref, g2_ref,
                 ln1g_ref, ln1b_ref, ln2g_ref, ln2b_ref, w_out_ref, w1_ref, w3_ref, w2_ref, o_ref, *, seq):
    tm = x_ref.shape[0]
    u = u_ref[...]
    row = lax.broadcasted_iota(jnp.int32, (tm, 1), 0)
    pos = (row + pl.program_id(0) * tm) % seq
    u_prev = jnp.where(row == 0, up_ref[7:8, :], pltpu.roll(u, 1, axis=0))
    u_prev = jnp.where(pos == 0, 0.0, u_prev)
    u_next = jnp.where(row == tm - 1, un_ref[0:1, :], pltpu.roll(u, tm - 1, axis=0))
    u_next = jnp.where(pos == seq - 1, 0.0, u_next)
    cw = cw_ref[0]
    y_a = ab_ref[...] * (u_prev * cw[0:1] + u * cw[1:2] + u_next * cw[2:3])

    y_cat = jnp.concatenate([y_a.astype(BF16), yb_ref[...], yc_ref[...]], axis=1)
    y = jnp.dot(y_cat, w_out_ref[...], preferred_element_type=F32)
    x1 = _layer_norm(DEEPNORM_ALPHA * x_ref[...] + g1_ref[0] * y, ln1g_ref[0], ln1b_ref[0])

    hf = (x1 * (1.0 + sc2_ref[0]) + sh2_ref[0]).astype(BF16)
    f = None
    for c in range(D_FF // FF_CHUNK):
        cols = slice(c * FF_CHUNK, (c + 1) * FF_CHUNK)
        g = jnp.dot(hf, w1_ref[:, cols], preferred_element_type=F32)
        a = (g * _sigmoid(g) * jnp.dot(hf, w3_ref[:, cols], preferred_element_type=F32)).astype(BF16)
        t = jnp.dot(a, w2_ref[cols, :], preferred_element_type=F32)
        f = t if f is None else f + t
    o_ref[...] = _layer_norm(DEEPNORM_ALPHA * x1 + g2_ref[0] * f, ln2g_ref[0], ln2b_ref[0])


def _tail_call(x2d, u, ab, yb, yc, mod3, wts, l, *, seq, tm, mod_row):
    rows = x2d.shape[0]
    n_tiles = rows // tm
    halo = 8
    n_halo = rows // halo

    def mod_spec(j):
        return pl.BlockSpec((1, 1, D_MODEL), lambda i: ((l * COND_ROWS + mod_row(i)) * N_MOD + j, 0, 0))

    def row_spec(width):
        return pl.BlockSpec((tm, width), lambda i: (i, 0))

    in_specs = [
        row_spec(D_MODEL), row_spec(CONV_W),
        pl.BlockSpec((halo, CONV_W), lambda i: (jnp.maximum(i * (tm // halo) - 1, 0), 0)),
        pl.BlockSpec((halo, CONV_W), lambda i: (jnp.minimum((i + 1) * (tm // halo), n_halo - 1), 0)),
        row_spec(CONV_W), row_spec(DIFF_WIDTH), row_spec(MLA_WIDTH),
        pl.BlockSpec((1, 3, CONV_W), lambda i: (l, 0, 0)),
        mod_spec(2), mod_spec(3), mod_spec(4), mod_spec(5),
        _layer_spec((1, D_MODEL), l), _layer_spec((1, D_MODEL), l),
        _layer_spec((1, D_MODEL), l), _layer_spec((1, D_MODEL), l),
        _layer_spec((D_MODEL, D_MODEL), l),
        _layer_spec((D_MODEL, D_FF), l), _layer_spec((D_MODEL, D_FF), l), _layer_spec((D_FF, D_MODEL), l),
    ]
    return pl.pallas_call(
        functools.partial(_tail_kernel, seq=seq),
        out_shape=jax.ShapeDtypeStruct((rows, D_MODEL), F32),
        grid=(n_tiles,),
        in_specs=in_specs,
        out_specs=row_spec(D_MODEL),
        compiler_params=_cparams(1), name=f"tail_l{l}",
    )(x2d, u, u, u, ab, yb, yc, wts["conv_w"], mod3, mod3, mod3, mod3,
      wts["ln1_g"], wts["ln1_b"], wts["ln2_g"], wts["ln2_b"],
      wts["w_out"], wts["w_ff1"], wts["w_ff3"], wts["w_ff2"])


def _rope_tables(n_tokens):
    rows = n_tokens // GRID_W
    row = np.repeat(np.arange(rows), GRID_W).astype(np.float32)
    col = np.tile(np.arange(GRID_W), rows).astype(np.float32)
    half = MLA_ROPE // 4
    freqs = (np.float32(ROPE_BASE) ** (-np.arange(half, dtype=np.float32) / np.float32(half))).astype(np.float32)
    ang_r = row[:, None] * freqs[None, :]
    ang_c = col[:, None] * freqs[None, :]
    cos32 = np.concatenate([np.cos(ang_r), np.cos(ang_r), np.cos(ang_c), np.cos(ang_c)], axis=1)
    sin32 = np.concatenate([-np.sin(ang_r), np.sin(ang_r), -np.sin(ang_c), np.sin(ang_c)], axis=1)
    one = np.ones((n_tokens, 1), np.float32)
    zero = np.zeros((n_tokens, 1), np.float32)
    cos_d, sin_d = np.tile(cos32, (1, 4)), np.tile(sin32, (1, 4))
    cos_m = np.concatenate([np.tile(one, (1, MLA_NOPE)), cos32, np.tile(one, (1, 32))], axis=1)
    sin_m = np.concatenate([np.tile(zero, (1, MLA_NOPE)), sin32, np.tile(zero, (1, 32))], axis=1)
    cos_p = np.concatenate([cos32, np.tile(one, (1, 96))], axis=1)
    sin_p = np.concatenate([sin32, np.tile(zero, (1, 96))], axis=1)
    return tuple(jnp.asarray(t.astype(np.float32)) for t in (cos_d, sin_d, cos_m, sin_m, cos_p, sin_p))


def _prep_weights(w_in, w_uq, w_ukv, w_out, w_ff1, w_ff3, w_ff2, conv_w, q_norm_w, kv_norm_w,
                  ln1_g, ln1_b, ln2_g, ln2_b):
    n_l = w_in.shape[0]
    w_in_p = jnp.pad(w_in, ((0, 0), (0, 0), (0, IN_COLS_PAD - IN_COLS))).astype(BF16)
    w_uq_p = jnp.pad(w_uq.reshape(n_l, MLA_Q_RANK, MLA_HEADS, MLA_QK),
                     ((0, 0), (0, 0), (0, 0), (0, MLA_HEAD_PAD - MLA_QK))).reshape(n_l, MLA_Q_RANK, MLA_QK_PAD)
    w_ukv4 = w_ukv.reshape(n_l, MLA_KV_RANK, MLA_HEADS, MLA_NOPE + MLA_V)
    w_uk_p = jnp.pad(w_ukv4[..., :MLA_NOPE], ((0, 0), (0, 0), (0, 0), (0, MLA_HEAD_PAD - MLA_NOPE)))
    w_kv = jnp.concatenate([w_uk_p.reshape(n_l, MLA_KV_RANK, MLA_QK_PAD),
                            w_ukv4[..., MLA_NOPE:].reshape(n_l, MLA_KV_RANK, MLA_WIDTH)], axis=-1)
    vec = lambda a: a.reshape(n_l, 1, a.shape[-1])
    return {
        "w_in": w_in_p, "w_uq": w_uq_p.astype(BF16), "w_kv": w_kv.astype(BF16),
        "w_out": w_out.astype(BF16), "w_ff1": w_ff1.astype(BF16), "w_ff3": w_ff3.astype(BF16),
        "w_ff2": w_ff2.astype(BF16), "conv_w": conv_w,
        "q_norm_w": vec(q_norm_w), "kv_norm_w": vec(kv_norm_w),
        "ln1_g": vec(ln1_g), "ln1_b": vec(ln1_b), "ln2_g": vec(ln2_g), "ln2_b": vec(ln2_b),
    }


def kernel(x_prompt, x_sample, cache_diff_k, cache_diff_v, cache_mla_ckv, cache_mla_kpe, c, c_ctx,
           w_ada, b_ada, w_in, conv_w, lam_q1, lam_k1, lam_q2, lam_k2, diff_norm_w, q_norm_w, w_uq,
           kv_norm_w, w_ukv, w_out, ln1_g, ln1_b, w_ff1, w_ff3, w_ff2, ln2_g, ln2_b):
    n_l = w_in.shape[0]
    pb, ps, _ = x_prompt.shape
    sb, ss, _ = x_sample.shape
    past = cache_mla_ckv.shape[2]

    wts = _prep_weights(w_in, w_uq, w_ukv, w_out, w_ff1, w_ff3, w_ff2, conv_w, q_norm_w, kv_norm_w,
                        ln1_g, ln1_b, ln2_g, ln2_b)
    lamv = jnp.stack([lam_q1, lam_k1, lam_q2, lam_k2], axis=1)
    dnw = jnp.tile(diff_norm_w, (1, DIFF_HEADS)).reshape(n_l, 1, DIFF_WIDTH)

    cond = jnp.concatenate([c_ctx[None, :], c, jnp.zeros((COND_ROWS - 1 - sb, D_MODEL), F32)], axis=0)
    mod3 = _ada_call(cond, w_ada, b_ada).reshape(n_l * COND_ROWS * N_MOD, 1, D_MODEL)

    ck_t = jnp.swapaxes(cache_diff_k, -1, -2).reshape(sb, n_l, DIFF_QK_COLS, past).astype(BF16)
    cv = jnp.transpose(cache_diff_v, (0, 1, 3, 2, 4)).reshape(sb, n_l, past, DIFF_WIDTH).astype(BF16)
    kpe_slab = jnp.pad(cache_mla_kpe, ((0, 0), (0, 0), (0, 0), (0, LANES - MLA_ROPE)))
    ckc_t, cvm = _cache_kv_call(cache_mla_ckv, kpe_slab, wts["w_kv"])

    rope_tabs = _rope_tables(ss)
    tm_p, tm_s, tq_s = 512, 512, 256

    xp = x_prompt.reshape(pb * ps, D_MODEL)
    xs = x_sample.reshape(sb * ss, D_MODEL)
    states = []
    for l in range(n_l):
        lam_init = 0.8 - 0.6 * math.exp(-0.3 * l)

        row_p = lambda i: 0
        (u, ab, qd, kt, vd, qm, kct, vm, st_k, st_v, st_ckv, st_kpe) = _inproj_call(
            xp, mod3, wts, l, n_batch=pb, seq=ps, tm=tm_p, mod_row=row_p, rope_tabs=None, with_state=True)
        yb, yc = _attn_call(lamv, dnw, qd, kt, vd, qm, kct, vm, None, l,
                            n_batch=pb, seq=ps, tq=ps, lam_init=lam_init)
        xp = _tail_call(xp, u, ab, yb, yc, mod3, wts, l, seq=ps, tm=tm_p, mod_row=row_p)
        states.append((st_k, st_v, st_ckv.reshape(pb, ps, MLA_KV_RANK), st_kpe.reshape(pb, ps, MLA_ROPE)))

        row_s = lambda i: 1 + i // (ss // tm_s)
        (u, ab, qd, kt, vd, qm, kct, vm) = _inproj_call(
            xs, mod3, wts, l, n_batch=sb, seq=ss, tm=tm_s, mod_row=row_s, rope_tabs=rope_tabs, with_state=False)
        yb, yc = _attn_call(lamv, dnw, qd, kt, vd, qm, kct, vm, (ck_t, cv, ckc_t, cvm), l,
                            n_batch=sb, seq=ss, tq=tq_s, lam_init=lam_init)
        xs = _tail_call(xs, u, ab, yb, yc, mod3, wts, l, seq=ss, tm=tm_s, mod_row=row_s)

    return (xp.reshape(pb, ps, D_MODEL), xs.reshape(sb, ss, D_MODEL),
            jnp.stack([s[0] for s in states], axis=1), jnp.stack([s[1] for s in states], axis=1),
            jnp.stack([s[2] for s in states], axis=1), jnp.stack([s[3] for s in states], axis=1))
```

```python
import functools
import math

import numpy as np
import jax
import jax.numpy as jnp
from jax import lax
from jax.experimental import pallas as pl
from jax.experimental.pallas import tpu as pltpu

F32 = jnp.float32
BF16 = jnp.bfloat16

D_MODEL = 1024
DEPTH = 2
GRID_W = 64
ROPE_BASE = 10000.0
CONV_W = 256
DIFF_HEADS = 4
DIFF_DK = 32
DIFF_DV = 64
DIFF_WIDTH = DIFF_HEADS * DIFF_DV
DIFF_QK_COLS = DIFF_HEADS * 2 * DIFF_DK
DIFF_SCALE = DIFF_DK ** -0.5
MLA_HEADS = 8
MLA_Q_RANK = 384
MLA_KV_RANK = 256
MLA_NOPE = 64
MLA_ROPE = 32
MLA_V = 64
MLA_QK = MLA_NOPE + MLA_ROPE
MLA_WIDTH = MLA_HEADS * MLA_V
MLA_SCALE = MLA_QK ** -0.5
IN_COLS = 3 * CONV_W + 2 * DIFF_QK_COLS + DIFF_WIDTH + MLA_Q_RANK + MLA_KV_RANK + MLA_ROPE
D_FF = 2816
DEEPNORM_ALPHA = (2 * DEPTH) ** 0.25
LOG2E = 1.4426950408889634

LANES = 128
SUBLANES = 8
BF16_ROWS = 16
MXU_DIM = 256
IN_COLS_PAD = 2304
MLA_HEAD_PAD = LANES
MLA_QK_PAD = MLA_HEADS * MLA_HEAD_PAD
N_MOD = 6
COND_ROWS = 8
FF_CHUNKS = ((0, 1536), (1536, 2816))
KEY_CHUNK = 64
VMEM_LIMIT = 56 * 2 ** 20

OFF_AX, OFF_AB, OFF_AC = 0, 256, 512
OFF_DQ, OFF_DK, OFF_DV = 768, 1024, 1280
OFF_CQ, OFF_CKV, OFF_KPE = 1536, 1920, 2176


def _cparams(n_grid):
    return pltpu.CompilerParams(dimension_semantics=("arbitrary",) * n_grid,
                                vmem_limit_bytes=VMEM_LIMIT)


def _layer_spec(tail, l):
    nd = len(tail)
    return pl.BlockSpec((None,) + tuple(tail), lambda *_: (l,) + (0,) * nd,
                        pipeline_mode=pl.Buffered(1))


def _sigmoid(x):
    return 1.0 / (1.0 + jnp.exp(-x))


def _layer_norm(x, g, b):
    mu = jnp.mean(x, axis=-1, keepdims=True)
    xc = x - mu
    var = jnp.mean(xc * xc, axis=-1, keepdims=True)
    return xc * lax.rsqrt(var + 1e-5) * g + b


def _rms_norm(x, w):
    ms = jnp.mean(x * x, axis=-1, keepdims=True)
    return x * lax.rsqrt(ms + 1e-6) * w


def _rope_slab(x, cos, sin_signed):
    lane = lax.broadcasted_iota(jnp.int32, x.shape, 1)
    lo = (lane % 16) < 8
    partner = jnp.where(lo, pltpu.roll(x, LANES - 8, axis=1), pltpu.roll(x, 8, axis=1))
    return x * cos + partner * sin_signed


def _ada_kernel(cond_ref, w_ref, b_ref, o_ref):
    c = cond_ref[...]
    s = (c * _sigmoid(c)).astype(BF16)
    o_ref[0] = jnp.dot(s, w_ref[0].astype(BF16), preferred_element_type=F32) + b_ref[0]


def _ada_call(cond, w_ada, b_ada):
    n_l = w_ada.shape[0]
    tn = D_MODEL
    return pl.pallas_call(
        _ada_kernel,
        out_shape=jax.ShapeDtypeStruct((n_l, COND_ROWS, N_MOD * D_MODEL), F32),
        grid=(n_l, N_MOD * D_MODEL // tn),
        in_specs=[
            pl.BlockSpec((COND_ROWS, D_MODEL), lambda l, j: (0, 0)),
            pl.BlockSpec((1, D_MODEL, tn), lambda l, j: (l, 0, j)),
            pl.BlockSpec((1, 1, tn), lambda l, j: (l, 0, j)),
        ],
        out_specs=pl.BlockSpec((1, COND_ROWS, tn), lambda l, j: (l, 0, j)),
        compiler_params=_cparams(2),
        name="ada_mod",
    )(cond, w_ada, b_ada.reshape(n_l, 1, N_MOD * D_MODEL))


def _expand_kv(ckv, kpe_slab, w_kv_ref):
    kv = jnp.dot(ckv.astype(BF16), w_kv_ref[...], preferred_element_type=F32)
    kpe_shift = pltpu.roll(kpe_slab, MLA_NOPE, axis=1)
    k_cat = jnp.concatenate(
        [(kv[:, h * MLA_HEAD_PAD:(h + 1) * MLA_HEAD_PAD] + kpe_shift).astype(BF16) for h in range(MLA_HEADS)],
        axis=1)
    return k_cat, kv[:, MLA_QK_PAD:MLA_QK_PAD + MLA_WIDTH]


def _cache_kv_kernel(ckv_ref, kpe_ref, w_kv_ref, kc_ref, vmt_ref):
    k_cat, v_m = _expand_kv(ckv_ref[...], kpe_ref[...], w_kv_ref)
    kc_ref[...] = k_cat
    vmt_ref[...] = v_m.T.astype(BF16)


def _cache_kv_call(cache_ckv, cache_kpe_slab, w_kv):
    n_b, n_l, past, _ = cache_ckv.shape
    return pl.pallas_call(
        _cache_kv_kernel,
        out_shape=(jax.ShapeDtypeStruct((n_b, n_l, past, MLA_QK_PAD), BF16),
                   jax.ShapeDtypeStruct((n_b, n_l, MLA_WIDTH, past), BF16)),
        grid=(n_b, n_l),
        in_specs=[
            pl.BlockSpec((None, None, past, MLA_KV_RANK), lambda b, l: (b, l, 0, 0)),
            pl.BlockSpec((None, None, past, LANES), lambda b, l: (b, l, 0, 0)),
            pl.BlockSpec((None, MLA_KV_RANK, MLA_QK_PAD + MLA_WIDTH), lambda b, l: (l, 0, 0)),
        ],
        out_specs=(pl.BlockSpec((None, None, past, MLA_QK_PAD), lambda b, l: (b, l, 0, 0)),
                   pl.BlockSpec((None, None, MLA_WIDTH, past), lambda b, l: (b, l, 0, 0))),
        compiler_params=_cparams(2),
        name="cache_kv",
    )(cache_ckv, cache_kpe_slab, w_kv)


N_STATE = 4


def _inproj_kernel(*refs, rope, n_seq, seq, with_state, alias_state):
    it = iter(refs)
    x_ref, sc_ref, sh_ref = next(it), next(it), next(it)
    w_in_ref, qnw_ref, w_uq_ref, kvnw_ref, w_kv_ref = next(it), next(it), next(it), next(it), next(it)
    if rope:
        cd_ref, sd_ref, cm_ref, sm_ref, cp_ref, sp_ref = (next(it) for _ in range(6))
    if alias_state:
        for _ in range(N_STATE):
            next(it)
    u_ref, ab_ref, qdt_ref, kd_ref, vdt_ref, qmt_ref, kc_ref, vmt_ref = (next(it) for _ in range(8))
    if with_state:
        stk_ref, stv_ref, stc_ref, stp_ref = (next(it) for _ in range(N_STATE))

    h = (x_ref[...] * (1.0 + sc_ref[0]) + sh_ref[0]).astype(BF16)
    proj = jnp.dot(h, w_in_ref[...], preferred_element_type=F32)

    u_ref[...] = proj[:, OFF_AC:OFF_AC + CONV_W] * proj[:, OFF_AX:OFF_AX + CONV_W]
    ab_ref[...] = proj[:, OFF_AB:OFF_AB + CONV_W]

    q = proj[:, OFF_DQ:OFF_DQ + DIFF_QK_COLS]
    k = proj[:, OFF_DK:OFF_DK + DIFF_QK_COLS]
    v = proj[:, OFF_DV:OFF_DV + DIFF_WIDTH]
    if rope:
        cd, sd = cd_ref[...], sd_ref[...]
        q = jnp.concatenate([_rope_slab(q[:, j * LANES:(j + 1) * LANES], cd, sd) for j in range(2)], axis=1)
        k = jnp.concatenate([_rope_slab(k[:, j * LANES:(j + 1) * LANES], cd, sd) for j in range(2)], axis=1)
    kd_ref[...] = k.astype(BF16)

    cq = _rms_norm(proj[:, OFF_CQ:OFF_CQ + MLA_Q_RANK], qnw_ref[0])
    qc = jnp.dot(cq.astype(BF16), w_uq_ref[...], preferred_element_type=F32)
    if rope:
        cm, sm = cm_ref[...], sm_ref[...]
        qc = jnp.concatenate(
            [_rope_slab(qc[:, hd * MLA_HEAD_PAD:(hd + 1) * MLA_HEAD_PAD], cm, sm) for hd in range(MLA_HEADS)],
            axis=1)

    ckv = _rms_norm(proj[:, OFF_CKV:OFF_CKV + MLA_KV_RANK], kvnw_ref[0])
    kpe_slab = proj[:, OFF_KPE:OFF_KPE + LANES]
    if rope:
        kpe_slab = _rope_slab(kpe_slab, cp_ref[...], sp_ref[...])
    k_cat, v_m = _expand_kv(ckv, kpe_slab, w_kv_ref)
    kc_ref[...] = k_cat

    q_t, v_t, qc_t, vm_t = q.T.astype(BF16), v.T.astype(BF16), qc.T.astype(BF16), v_m.T.astype(BF16)
    for j in range(max(n_seq, 1)):
        cols = slice(j * seq, (j + 1) * seq) if n_seq else slice(None)
        for dst, val in ((qdt_ref, q_t), (vdt_ref, v_t), (qmt_ref, qc_t), (vmt_ref, vm_t)):
            (dst.at[j] if n_seq else dst)[...] = val[:, cols]

    if with_state:
        for j in range(n_seq):
            rows = slice(j * seq, (j + 1) * seq)
            stc_ref[j] = ckv[rows]
            stp_ref[j] = kpe_slab[rows, :MLA_ROPE]
            for hm in range(2 * DIFF_HEADS):
                stk_ref[j, hm // 2, hm % 2] = k[rows, hm * DIFF_DK:(hm + 1) * DIFF_DK]
            for hd in range(DIFF_HEADS):
                stv_ref[j, hd] = v[rows, hd * DIFF_DV:(hd + 1) * DIFF_DV]


def _inproj_call(x2d, mod3, wts, l, *, n_batch, seq, tm, mod_row, rope_tabs, n_layers, states):
    rows = n_batch * seq
    n_tiles = rows // tm
    rope = rope_tabs is not None
    with_state = states is not None
    alias_state = bool(states)
    n_seq = tm // seq
    t_per_seq = max(seq // tm, 1)

    def mod_spec(j):
        return pl.BlockSpec((1, 1, D_MODEL), lambda i: ((l * COND_ROWS + mod_row(i)) * N_MOD + j, 0, 0))

    def row_spec(width):
        return pl.BlockSpec((tm, width), lambda i: (i, 0))

    def feat_spec(width):
        if n_seq:
            return pl.BlockSpec((n_seq, width, seq), lambda i: (i, 0, 0))
        return pl.BlockSpec((None, width, tm), lambda i: (i // t_per_seq, 0, i % t_per_seq))

    in_specs = [
        row_spec(D_MODEL), mod_spec(1), mod_spec(0),
        _layer_spec((D_MODEL, IN_COLS_PAD), l),
        _layer_spec((1, MLA_Q_RANK), l),
        _layer_spec((MLA_Q_RANK, MLA_QK_PAD), l),
        _layer_spec((1, MLA_KV_RANK), l),
        _layer_spec((MLA_KV_RANK, MLA_QK_PAD + MLA_WIDTH), l),
    ]
    args = [x2d, mod3, mod3, wts["w_in"], wts["q_norm_w"], wts["w_uq"], wts["kv_norm_w"], wts["w_kv"]]
    if rope:
        in_specs += [pl.BlockSpec((tm, LANES), lambda i: (i % t_per_seq, 0))] * 6
        args += list(rope_tabs)

    out_shape = [
        jax.ShapeDtypeStruct((rows, CONV_W), F32),
        jax.ShapeDtypeStruct((rows, CONV_W), F32),
        jax.ShapeDtypeStruct((n_batch, DIFF_QK_COLS, seq), BF16),
        jax.ShapeDtypeStruct((rows, DIFF_QK_COLS), BF16),
        jax.ShapeDtypeStruct((n_batch, DIFF_WIDTH, seq), BF16),
        jax.ShapeDtypeStruct((n_batch, MLA_QK_PAD, seq), BF16),
        jax.ShapeDtypeStruct((rows, MLA_QK_PAD), BF16),
        jax.ShapeDtypeStruct((n_batch, MLA_WIDTH, seq), BF16),
    ]
    out_specs = [row_spec(CONV_W), row_spec(CONV_W), feat_spec(DIFF_QK_COLS), row_spec(DIFF_QK_COLS),
                 feat_spec(DIFF_WIDTH), feat_spec(MLA_QK_PAD), row_spec(MLA_QK_PAD), feat_spec(MLA_WIDTH)]
    aliases = {}
    if with_state:
        assert n_seq >= 1
        n_out = len(out_shape)
        out_shape += [
            jax.ShapeDtypeStruct((n_batch, n_layers, DIFF_HEADS, 2, seq, DIFF_DK), F32),
            jax.ShapeDtypeStruct((n_batch, n_layers, DIFF_HEADS, seq, DIFF_DV), F32),
            jax.ShapeDtypeStruct((n_batch, n_layers, seq, MLA_KV_RANK), F32),
            jax.ShapeDtypeStruct((n_batch, n_layers, seq, MLA_ROPE), F32),
        ]
        out_specs += [
            pl.BlockSpec((n_seq, None, DIFF_HEADS, 2, seq, DIFF_DK), lambda i: (i, l, 0, 0, 0, 0)),
            pl.BlockSpec((n_seq, None, DIFF_HEADS, seq, DIFF_DV), lambda i: (i, l, 0, 0, 0)),
            pl.BlockSpec((n_seq, None, seq, MLA_KV_RANK), lambda i: (i, l, 0, 0)),
            pl.BlockSpec((n_seq, None, seq, MLA_ROPE), lambda i: (i, l, 0, 0)),
        ]
        if alias_state:
            for j, st in enumerate(states):
                aliases[len(args)] = n_out + j
                in_specs.append(pl.BlockSpec(memory_space=pl.ANY))
                args.append(st)

    kern = functools.partial(_inproj_kernel, rope=rope, n_seq=n_seq, seq=seq, with_state=with_state,
                             alias_state=alias_state)
    return pl.pallas_call(
        kern, out_shape=tuple(out_shape), grid=(n_tiles,),
        in_specs=in_specs, out_specs=tuple(out_specs), input_output_aliases=aliases,
        compiler_params=_cparams(1), name=f"inproj_l{l}_{'s' if rope else 'p'}",
    )(*args)


def _tree(op, xs):
    xs = list(xs)
    while len(xs) > 1:
        xs = [op(xs[i], xs[i + 1]) if i + 1 < len(xs) else xs[i] for i in range(0, len(xs), 2)]
    return xs[0]


def _slabs(x):
    return [x[j * SUBLANES:(j + 1) * SUBLANES] for j in range(x.shape[0] // SUBLANES)]


class _Head:
    def __init__(self, k_refs, rhs_fn, vt_refs, scale_log2e, slot):
        self.k_refs, self.rhs_fn, self.vt_refs, self.scale, self.slot = k_refs, rhs_fn, vt_refs, scale_log2e, slot


def _scores(head, s_scr):
    rhs = head.rhs_fn()
    m8, off = None, 0
    for k_ref in head.k_refs:
        n = k_ref.shape[0]
        s = jnp.dot(k_ref[...], rhs, preferred_element_type=F32)
        s_scr[head.slot, off:off + n, :] = s
        blk = _tree(jnp.maximum, _slabs(s))
        m8 = blk if m8 is None else jnp.maximum(m8, blk)
        off += n
    return jnp.max(m8, axis=0, keepdims=True)


def _probs(head, m, s_scr, p_scr):
    for c0 in range(0, s_scr.shape[1], KEY_CHUNK):
        p = jnp.exp2((s_scr[head.slot, c0:c0 + KEY_CHUNK, :] - m) * head.scale)
        p_scr[head.slot, c0:c0 + KEY_CHUNK, :] = p.astype(BF16)


def _weighted_values(head, p_scr):
    o, off = None, 0
    for vt_ref in head.vt_refs:
        n = vt_ref.shape[1]
        v_ext = jnp.concatenate([vt_ref[...], jnp.ones((BF16_ROWS, n), BF16)], axis=0)
        pv = jnp.dot(v_ext, p_scr[head.slot, off:off + n, :], preferred_element_type=F32)
        o = pv if o is None else o + pv
        off += n
    dv = o.shape[0] - BF16_ROWS
    return o[:dv] * (1.0 / o[dv:dv + 1])


def _run_heads(heads, s_scr, p_scr, finish):
    m = _scores(heads[0], s_scr)
    for t, head in enumerate(heads):
        m_next = _scores(heads[t + 1], s_scr) if t + 1 < len(heads) else None
        _probs(head, m, s_scr, p_scr)
        finish(t, _weighted_values(head, p_scr))
        m = m_next


def _attn_kernel(*refs, lam_init, has_cache):
    it = iter(refs)
    lam_ref, dnw_ref = next(it), next(it)
    qdt_ref, kd_ref, vdt_ref, qmt_ref, kc_ref, vmt_ref = (next(it) for _ in range(6))
    kds, vdts, kcs, vmts = [kd_ref], [vdt_ref], [kc_ref], [vmt_ref]
    if has_cache:
        kds.append(next(it)); vdts.append(next(it)); kcs.append(next(it)); vmts.append(next(it))
    yb_ref, yc_ref = next(it), next(it)
    s_scr, p_scr = next(it), next(it)

    lv = lam_ref[0]
    lam = (jnp.exp(jnp.sum(lv[0:1] * lv[1:2], axis=-1, keepdims=True))
           - jnp.exp(jnp.sum(lv[2:3] * lv[3:4], axis=-1, keepdims=True)) + lam_init)

    def diff_rhs(hm):
        def fn():
            q_dt = qdt_ref[...]
            feat = lax.broadcasted_iota(jnp.int32, q_dt.shape, 0)
            return jnp.where((feat >= hm * DIFF_DK) & (feat < (hm + 1) * DIFF_DK), q_dt, jnp.zeros_like(q_dt))
        return fn

    def mla_rhs(feats):
        return lambda: qmt_ref[feats, :]

    heads = []
    for hm in range(2 * DIFF_HEADS):
        rows = slice((hm // 2) * DIFF_DV, (hm // 2 + 1) * DIFF_DV)
        heads.append(_Head(kds, diff_rhs(hm), [vt.at[rows, :] for vt in vdts], DIFF_SCALE * LOG2E, len(heads) % 2))
    for hd in range(MLA_HEADS):
        feats = slice(hd * MLA_HEAD_PAD, (hd + 1) * MLA_HEAD_PAD)
        rows = slice(hd * MLA_V, (hd + 1) * MLA_V)
        heads.append(_Head([kc.at[:, feats] for kc in kcs], mla_rhs(feats), [vt.at[rows, :] for vt in vmts],
                           MLA_SCALE * LOG2E, len(heads) % 2))

    outs = []

    def finish(t, o):
        outs.append(o)
        if t == 2 * DIFF_HEADS - 1:
            y_heads = []
            for hd in range(DIFF_HEADS):
                d = outs[2 * hd] - lam * outs[2 * hd + 1]
                ms = jnp.mean(d * d, axis=0, keepdims=True)
                y_heads.append(d * lax.rsqrt(ms + 1e-6))
            y_b = jnp.concatenate(y_heads, axis=0).T
            yb_ref[...] = (y_b * dnw_ref[0] * (1.0 - lam_init)).astype(BF16)
        if t == len(heads) - 1:
            yc_ref[...] = jnp.concatenate(outs[2 * DIFF_HEADS:], axis=0).T.astype(BF16)

    _run_heads(heads, s_scr, p_scr, finish)


def _attn_call(lamv, dnw, qdt, kd, vdt, qmt, kc, vmt, cache, l, *, n_batch, seq, tq, lam_init):
    t_per_seq = seq // tq
    has_cache = cache is not None

    def q_spec(width):
        return pl.BlockSpec((None, width, tq), lambda b, t: (b, 0, t))

    def out_spec(width):
        return pl.BlockSpec((tq, width), lambda b, t: (b * t_per_seq + t, 0))

    in_specs = [
        pl.BlockSpec((1, 4, DIFF_DK), lambda b, t: (l, 0, 0)),
        pl.BlockSpec((1, 1, DIFF_WIDTH), lambda b, t: (l, 0, 0)),
        q_spec(DIFF_QK_COLS),
        pl.BlockSpec((seq, DIFF_QK_COLS), lambda b, t: (b, 0)),
        pl.BlockSpec((None, DIFF_WIDTH, seq), lambda b, t: (b, 0, 0)),
        q_spec(MLA_QK_PAD),
        pl.BlockSpec((seq, MLA_QK_PAD), lambda b, t: (b, 0)),
        pl.BlockSpec((None, MLA_WIDTH, seq), lambda b, t: (b, 0, 0)),
    ]
    args = [lamv, dnw, qdt, kd, vdt, qmt, kc, vmt]
    past = 0
    if has_cache:
        past = cache[0].shape[2]
        in_specs += [
            pl.BlockSpec((None, None, past, DIFF_QK_COLS), lambda b, t: (b, l, 0, 0)),
            pl.BlockSpec((None, None, DIFF_WIDTH, past), lambda b, t: (b, l, 0, 0)),
            pl.BlockSpec((None, None, past, MLA_QK_PAD), lambda b, t: (b, l, 0, 0)),
            pl.BlockSpec((None, None, MLA_WIDTH, past), lambda b, t: (b, l, 0, 0)),
        ]
        args += list(cache)
    rows = n_batch * seq
    nk = seq + past
    return pl.pallas_call(
        functools.partial(_attn_kernel, lam_init=lam_init, has_cache=has_cache),
        out_shape=(jax.ShapeDtypeStruct((rows, DIFF_WIDTH), BF16),
                   jax.ShapeDtypeStruct((rows, MLA_WIDTH), BF16)),
        grid=(n_batch, t_per_seq),
        in_specs=in_specs,
        out_specs=(out_spec(DIFF_WIDTH), out_spec(MLA_WIDTH)),
        scratch_shapes=[pltpu.VMEM((2, nk, tq), F32), pltpu.VMEM((2, nk, tq), BF16)],
        compiler_params=_cparams(2), name=f"attn_l{l}_{'s' if has_cache else 'p'}",
    )(*args)


def _tail_kernel(x_ref, u_ref, up_ref, un_ref, ab_ref, yb_ref, yc_ref, cw_ref, g1_ref, sh2_ref, sc2_ref, g2_ref,
                 ln1g_ref, ln1b_ref, ln2g_ref, ln2b_ref, w_out_ref, w1_ref, w3_ref, w2_ref, o_ref, *, seq):
    tm = x_ref.shape[0]
    u = u_ref[...]
    row = lax.broadcasted_iota(jnp.int32, (tm, 1), 0)
    pos = (row + pl.program_id(0) * tm) % seq
    u_prev = jnp.where(row == 0, up_ref[SUBLANES - 1:SUBLANES, :], pltpu.roll(u, 1, axis=0))
    u_prev = jnp.where(pos == 0, 0.0, u_prev)
    u_next = jnp.where(row == tm - 1, un_ref[0:1, :], pltpu.roll(u, tm - 1, axis=0))
    u_next = jnp.where(pos == seq - 1, 0.0, u_next)
    cw = cw_ref[0]
    y_a = ab_ref[...] * (u_prev * cw[0:1] + u * cw[1:2] + u_next * cw[2:3])

    y_cat = jnp.concatenate([y_a.astype(BF16), yb_ref[...], yc_ref[...]], axis=1)
    y = jnp.dot(y_cat, w_out_ref[...], preferred_element_type=F32)
    x1 = _layer_norm(DEEPNORM_ALPHA * x_ref[...] + g1_ref[0] * y, ln1g_ref[0], ln1b_ref[0])

    hf = (x1 * (1.0 + sc2_ref[0]) + sh2_ref[0]).astype(BF16)
    f = None
    for c0, c1 in FF_CHUNKS:
        g = jnp.dot(hf, w1_ref[:, c0:c1], preferred_element_type=F32)
        a = (g * _sigmoid(g) * jnp.dot(hf, w3_ref[:, c0:c1], preferred_element_type=F32)).astype(BF16)
        t = jnp.dot(a, w2_ref[c0:c1, :], preferred_element_type=F32)
        f = t if f is None else f + t
    o_ref[...] = _layer_norm(DEEPNORM_ALPHA * x1 + g2_ref[0] * f, ln2g_ref[0], ln2b_ref[0])


def _tail_call(x2d, u, ab, yb, yc, mod3, wts, l, *, seq, tm, mod_row):
    rows = x2d.shape[0]
    n_tiles = rows // tm
    n_halo = rows // SUBLANES

    def mod_spec(j):
        return pl.BlockSpec((1, 1, D_MODEL), lambda i: ((l * COND_ROWS + mod_row(i)) * N_MOD + j, 0, 0))

    def row_spec(width):
        return pl.BlockSpec((tm, width), lambda i: (i, 0))

    in_specs = [
        row_spec(D_MODEL), row_spec(CONV_W),
        pl.BlockSpec((SUBLANES, CONV_W), lambda i: (jnp.maximum(i * (tm // SUBLANES) - 1, 0), 0)),
        pl.BlockSpec((SUBLANES, CONV_W), lambda i: (jnp.minimum((i + 1) * (tm // SUBLANES), n_halo - 1), 0)),
        row_spec(CONV_W), row_spec(DIFF_WIDTH), row_spec(MLA_WIDTH),
        pl.BlockSpec((1, 3, CONV_W), lambda i: (l, 0, 0)),
        mod_spec(2), mod_spec(3), mod_spec(4), mod_spec(5),
        _layer_spec((1, D_MODEL), l), _layer_spec((1, D_MODEL), l),
        _layer_spec((1, D_MODEL), l), _layer_spec((1, D_MODEL), l),
        _layer_spec((D_MODEL, D_MODEL), l),
        _layer_spec((D_MODEL, D_FF), l), _layer_spec((D_MODEL, D_FF), l), _layer_spec((D_FF, D_MODEL), l),
    ]
    return pl.pallas_call(
        functools.partial(_tail_kernel, seq=seq),
        out_shape=jax.ShapeDtypeStruct((rows, D_MODEL), F32),
        grid=(n_tiles,),
        in_specs=in_specs,
        out_specs=row_spec(D_MODEL),
        compiler_params=_cparams(1), name=f"tail_l{l}",
    )(x2d, u, u, u, ab, yb, yc, wts["conv_w"], mod3, mod3, mod3, mod3,
      wts["ln1_g"], wts["ln1_b"], wts["ln2_g"], wts["ln2_b"],
      wts["w_out"], wts["w_ff1"], wts["w_ff3"], wts["w_ff2"])


def _rope_tables(n_tokens):
    rows = n_tokens // GRID_W
    row = np.repeat(np.arange(rows), GRID_W).astype(np.float32)
    col = np.tile(np.arange(GRID_W), rows).astype(np.float32)
    half = MLA_ROPE // 4
    freqs = (np.float32(ROPE_BASE) ** (-np.arange(half, dtype=np.float32) / np.float32(half))).astype(np.float32)
    ang_r = row[:, None] * freqs[None, :]
    ang_c = col[:, None] * freqs[None, :]
    cos32 = np.concatenate([np.cos(ang_r), np.cos(ang_r), np.cos(ang_c), np.cos(ang_c)], axis=1)
    sin32 = np.concatenate([-np.sin(ang_r), np.sin(ang_r), -np.sin(ang_c), np.sin(ang_c)], axis=1)
    one = np.ones((n_tokens, 1), np.float32)
    zero = np.zeros((n_tokens, 1), np.float32)
    cos_d, sin_d = np.tile(cos32, (1, 4)), np.tile(sin32, (1, 4))
    cos_m = np.concatenate([np.tile(one, (1, MLA_NOPE)), cos32, np.tile(one, (1, 32))], axis=1)
    sin_m = np.concatenate([np.tile(zero, (1, MLA_NOPE)), sin32, np.tile(zero, (1, 32))], axis=1)
    cos_p = np.concatenate([cos32, np.tile(one, (1, 96))], axis=1)
    sin_p = np.concatenate([sin32, np.tile(zero, (1, 96))], axis=1)
    return tuple(jnp.asarray(t.astype(np.float32)) for t in (cos_d, sin_d, cos_m, sin_m, cos_p, sin_p))


def _prep_weights(w_in, w_uq, w_ukv, w_out, w_ff1, w_ff3, w_ff2, conv_w, q_norm_w, kv_norm_w,
                  ln1_g, ln1_b, ln2_g, ln2_b):
    n_l = w_in.shape[0]
    w_in_p = jnp.pad(w_in, ((0, 0), (0, 0), (0, IN_COLS_PAD - IN_COLS))).astype(BF16)
    w_uq_p = jnp.pad(w_uq.reshape(n_l, MLA_Q_RANK, MLA_HEADS, MLA_QK),
                     ((0, 0), (0, 0), (0, 0), (0, MLA_HEAD_PAD - MLA_QK))).reshape(n_l, MLA_Q_RANK, MLA_QK_PAD)
    w_ukv4 = w_ukv.reshape(n_l, MLA_KV_RANK, MLA_HEADS, MLA_NOPE + MLA_V)
    w_uk_p = jnp.pad(w_ukv4[..., :MLA_NOPE], ((0, 0), (0, 0), (0, 0), (0, MLA_HEAD_PAD - MLA_NOPE)))
    w_kv = jnp.concatenate([w_uk_p.reshape(n_l, MLA_KV_RANK, MLA_QK_PAD),
                            w_ukv4[..., MLA_NOPE:].reshape(n_l, MLA_KV_RANK, MLA_WIDTH)], axis=-1)
    vec = lambda a: a.reshape(n_l, 1, a.shape[-1])
    return {
        "w_in": w_in_p, "w_uq": w_uq_p.astype(BF16), "w_kv": w_kv.astype(BF16),
        "w_out": w_out.astype(BF16), "w_ff1": w_ff1.astype(BF16), "w_ff3": w_ff3.astype(BF16),
        "w_ff2": w_ff2.astype(BF16), "conv_w": conv_w,
        "q_norm_w": vec(q_norm_w), "kv_norm_w": vec(kv_norm_w),
        "ln1_g": vec(ln1_g), "ln1_b": vec(ln1_b), "ln2_g": vec(ln2_g), "ln2_b": vec(ln2_b),
    }


def kernel(x_prompt, x_sample, cache_diff_k, cache_diff_v, cache_mla_ckv, cache_mla_kpe, c, c_ctx,
           w_ada, b_ada, w_in, conv_w, lam_q1, lam_k1, lam_q2, lam_k2, diff_norm_w, q_norm_w, w_uq,
           kv_norm_w, w_ukv, w_out, ln1_g, ln1_b, w_ff1, w_ff3, w_ff2, ln2_g, ln2_b):
    n_l = w_in.shape[0]
    pb, ps, _ = x_prompt.shape
    sb, ss, _ = x_sample.shape
    past = cache_mla_ckv.shape[2]

    wts = _prep_weights(w_in, w_uq, w_ukv, w_out, w_ff1, w_ff3, w_ff2, conv_w, q_norm_w, kv_norm_w,
                        ln1_g, ln1_b, ln2_g, ln2_b)
    lamv = jnp.stack([lam_q1, lam_k1, lam_q2, lam_k2], axis=1)
    dnw = jnp.tile(diff_norm_w, (1, DIFF_HEADS)).reshape(n_l, 1, DIFF_WIDTH)

    cond = jnp.concatenate([c_ctx[None, :], c, jnp.zeros((COND_ROWS - 1 - sb, D_MODEL), F32)], axis=0)
    mod3 = _ada_call(cond, w_ada, b_ada).reshape(n_l * COND_ROWS * N_MOD, 1, D_MODEL)

    ck = jnp.transpose(cache_diff_k, (0, 1, 4, 2, 3, 5)).reshape(sb, n_l, past, DIFF_QK_COLS).astype(BF16)
    cv_t = jnp.swapaxes(cache_diff_v, -1, -2).reshape(sb, n_l, DIFF_WIDTH, past).astype(BF16)
    kpe_slab = jnp.pad(cache_mla_kpe, ((0, 0), (0, 0), (0, 0), (0, LANES - MLA_ROPE)))
    ckc, cvm_t = _cache_kv_call(cache_mla_ckv, kpe_slab, wts["w_kv"])

    rope_tabs = _rope_tables(ss)
    tm_p, tm_s, tq_s = 512, 512, 512

    xp = x_prompt.reshape(pb * ps, D_MODEL)
    xs = x_sample.reshape(sb * ss, D_MODEL)
    states = (jnp.zeros((pb, n_l, DIFF_HEADS, 2, ps, DIFF_DK), F32),
              jnp.zeros((pb, n_l, DIFF_HEADS, ps, DIFF_DV), F32),
              jnp.zeros((pb, n_l, ps, MLA_KV_RANK), F32),
              jnp.zeros((pb, n_l, ps, MLA_ROPE), F32))
    for l in range(n_l):
        lam_init = 0.8 - 0.6 * math.exp(-0.3 * l)

        row_p = lambda i: 0
        outs = _inproj_call(xp, mod3, wts, l, n_batch=pb, seq=ps, tm=tm_p, mod_row=row_p, rope_tabs=None,
                            n_layers=n_l, states=states)
        u, ab, qdt, kd, vdt, qmt, kc, vmt = outs[:8]
        states = tuple(outs[8:])
        yb, yc = _attn_call(lamv, dnw, qdt, kd, vdt, qmt, kc, vmt, None, l,
                            n_batch=pb, seq=ps, tq=ps, lam_init=lam_init)
        xp = _tail_call(xp, u, ab, yb, yc, mod3, wts, l, seq=ps, tm=tm_p, mod_row=row_p)

        row_s = lambda i: 1 + i // (ss // tm_s)
        u, ab, qdt, kd, vdt, qmt, kc, vmt = _inproj_call(
            xs, mod3, wts, l, n_batch=sb, seq=ss, tm=tm_s, mod_row=row_s, rope_tabs=rope_tabs,
            n_layers=n_l, states=None)
        yb, yc = _attn_call(lamv, dnw, qdt, kd, vdt, qmt, kc, vmt, (ck, cv_t, ckc, cvm_t), l,
                            n_batch=sb, seq=ss, tq=tq_s, lam_init=lam_init)
        xs = _tail_call(xs, u, ab, yb, yc, mod3, wts, l, seq=ss, tm=tm_s, mod_row=row_s)

    return (xp.reshape(pb, ps, D_MODEL), xs.reshape(sb, ss, D_MODEL)) + states
```

```python
import functools
import math

import numpy as np
import jax
import jax.numpy as jnp
from jax import lax
from jax.experimental import pallas as pl
from jax.experimental.pallas import tpu as pltpu

F32 = jnp.float32
BF16 = jnp.bfloat16

D_MODEL = 1024
DEPTH = 2
GRID_W = 64
ROPE_BASE = 10000.0
CONV_W = 256
DIFF_HEADS = 4
DIFF_DK = 32
DIFF_DV = 64
DIFF_WIDTH = DIFF_HEADS * DIFF_DV
DIFF_QK_COLS = DIFF_HEADS * 2 * DIFF_DK
DIFF_SCALE = DIFF_DK ** -0.5
MLA_HEADS = 8
MLA_Q_RANK = 384
MLA_KV_RANK = 256
MLA_NOPE = 64
MLA_ROPE = 32
MLA_V = 64
MLA_QK = MLA_NOPE + MLA_ROPE
MLA_WIDTH = MLA_HEADS * MLA_V
MLA_SCALE = MLA_QK ** -0.5
IN_COLS = 3 * CONV_W + 2 * DIFF_QK_COLS + DIFF_WIDTH + MLA_Q_RANK + MLA_KV_RANK + MLA_ROPE
D_FF = 2816
DEEPNORM_ALPHA = (2 * DEPTH) ** 0.25
LOG2E = 1.4426950408889634

LANES = 128
SUBLANES = 8
BF16_ROWS = 16
MXU_DIM = 256
IN_COLS_PAD = 2304
MLA_HEAD_PAD = LANES
MLA_QK_PAD = MLA_HEADS * MLA_HEAD_PAD
N_MOD = 6
COND_ROWS = 8
FF_CHUNKS = ((0, 1536), (1536, 2816))
KEY_CHUNK = 64
VMEM_LIMIT = 56 * 2 ** 20

OFF_AX, OFF_AB, OFF_AC = 0, 256, 512
OFF_DQ, OFF_DK, OFF_DV = 768, 1024, 1280
OFF_CQ, OFF_CKV, OFF_KPE = 1536, 1920, 2176


def _cparams(n_grid):
    return pltpu.CompilerParams(dimension_semantics=("arbitrary",) * n_grid,
                                vmem_limit_bytes=VMEM_LIMIT)


def _layer_spec(tail, l):
    nd = len(tail)
    return pl.BlockSpec((None,) + tuple(tail), lambda *_: (l,) + (0,) * nd,
                        pipeline_mode=pl.Buffered(1))


def _sigmoid(x):
    return 1.0 / (1.0 + jnp.exp(-x))


def _layer_norm(x, g, b):
    mu = jnp.mean(x, axis=-1, keepdims=True)
    xc = x - mu
    var = jnp.mean(xc * xc, axis=-1, keepdims=True)
    return xc * lax.rsqrt(var + 1e-5) * g + b


def _rms_norm(x, w):
    ms = jnp.mean(x * x, axis=-1, keepdims=True)
    return x * lax.rsqrt(ms + 1e-6) * w


def _rope_slab(x, cos, sin_signed):
    lane = lax.broadcasted_iota(jnp.int32, x.shape, 1)
    lo = (lane % 16) < 8
    partner = jnp.where(lo, pltpu.roll(x, LANES - 8, axis=1), pltpu.roll(x, 8, axis=1))
    return x * cos + partner * sin_signed


def _ada_kernel(cond_ref, w_ref, b_ref, o_ref):
    c = cond_ref[...]
    s = (c * _sigmoid(c)).astype(BF16)
    o_ref[0] = jnp.dot(s, w_ref[0].astype(BF16), preferred_element_type=F32) + b_ref[0]


def _ada_call(cond, w_ada, b_ada):
    n_l = w_ada.shape[0]
    tn = D_MODEL
    return pl.pallas_call(
        _ada_kernel,
        out_shape=jax.ShapeDtypeStruct((n_l, COND_ROWS, N_MOD * D_MODEL), F32),
        grid=(n_l, N_MOD * D_MODEL // tn),
        in_specs=[
            pl.BlockSpec((COND_ROWS, D_MODEL), lambda l, j: (0, 0)),
            pl.BlockSpec((1, D_MODEL, tn), lambda l, j: (l, 0, j)),
            pl.BlockSpec((1, 1, tn), lambda l, j: (l, 0, j)),
        ],
        out_specs=pl.BlockSpec((1, COND_ROWS, tn), lambda l, j: (l, 0, j)),
        compiler_params=_cparams(2),
        name="ada_mod",
    )(cond, w_ada, b_ada.reshape(n_l, 1, N_MOD * D_MODEL))


def _expand_kv(ckv, kpe_slab, w_kv_ref):
    kv = jnp.dot(ckv.astype(BF16), w_kv_ref[...], preferred_element_type=F32)
    kpe_shift = pltpu.roll(kpe_slab, MLA_NOPE, axis=1)
    k_cat = jnp.concatenate(
        [(kv[:, h * MLA_HEAD_PAD:(h + 1) * MLA_HEAD_PAD] + kpe_shift).astype(BF16) for h in range(MLA_HEADS)],
        axis=1)
    return k_cat, kv[:, MLA_QK_PAD:MLA_QK_PAD + MLA_WIDTH]


def _cache_kv_kernel(ckv_ref, kpe_ref, w_kv_ref, kc_ref, vmt_ref):
    k_cat, v_m = _expand_kv(ckv_ref[...], kpe_ref[...], w_kv_ref)
    kc_ref[...] = k_cat
    vmt_ref[...] = v_m.T.astype(BF16)


def _cache_kv_call(cache_ckv, cache_kpe_slab, w_kv):
    n_b, n_l, past, _ = cache_ckv.shape
    return pl.pallas_call(
        _cache_kv_kernel,
        out_shape=(jax.ShapeDtypeStruct((n_b, n_l, past, MLA_QK_PAD), BF16),
                   jax.ShapeDtypeStruct((n_b, n_l, MLA_WIDTH, past), BF16)),
        grid=(n_b, n_l),
        in_specs=[
            pl.BlockSpec((None, None, past, MLA_KV_RANK), lambda b, l: (b, l, 0, 0)),
            pl.BlockSpec((None, None, past, LANES), lambda b, l: (b, l, 0, 0)),
            pl.BlockSpec((None, MLA_KV_RANK, MLA_QK_PAD + MLA_WIDTH), lambda b, l: (l, 0, 0)),
        ],
        out_specs=(pl.BlockSpec((None, None, past, MLA_QK_PAD), lambda b, l: (b, l, 0, 0)),
                   pl.BlockSpec((None, None, MLA_WIDTH, past), lambda b, l: (b, l, 0, 0))),
        compiler_params=_cparams(2),
        name="cache_kv",
    )(cache_ckv, cache_kpe_slab, w_kv)


N_STATE = 4


def _inproj_kernel(*refs, rope, n_seq, seq, with_state, alias_state):
    it = iter(refs)
    x_ref, sc_ref, sh_ref = next(it), next(it), next(it)
    w_in_ref, qnw_ref, w_uq_ref, kvnw_ref, w_kv_ref = next(it), next(it), next(it), next(it), next(it)
    if rope:
        cd_ref, sd_ref, cm_ref, sm_ref, cp_ref, sp_ref = (next(it) for _ in range(6))
    if alias_state:
        for _ in range(N_STATE):
            next(it)
    u_ref, ab_ref, qdt_ref, kd_ref, vdt_ref, qmt_ref, kc_ref, vmt_ref = (next(it) for _ in range(8))
    if with_state:
        stk_ref, stv_ref, stc_ref, stp_ref = (next(it) for _ in range(N_STATE))

    h = (x_ref[...] * (1.0 + sc_ref[0]) + sh_ref[0]).astype(BF16)
    proj = jnp.dot(h, w_in_ref[...], preferred_element_type=F32)

    u_ref[...] = proj[:, OFF_AC:OFF_AC + CONV_W] * proj[:, OFF_AX:OFF_AX + CONV_W]
    ab_ref[...] = proj[:, OFF_AB:OFF_AB + CONV_W]

    q = proj[:, OFF_DQ:OFF_DQ + DIFF_QK_COLS]
    k = proj[:, OFF_DK:OFF_DK + DIFF_QK_COLS]
    v = proj[:, OFF_DV:OFF_DV + DIFF_WIDTH]
    if rope:
        cd, sd = cd_ref[...], sd_ref[...]
        q = jnp.concatenate([_rope_slab(q[:, j * LANES:(j + 1) * LANES], cd, sd) for j in range(2)], axis=1)
        k = jnp.concatenate([_rope_slab(k[:, j * LANES:(j + 1) * LANES], cd, sd) for j in range(2)], axis=1)
    kd_ref[...] = k.astype(BF16)

    cq = _rms_norm(proj[:, OFF_CQ:OFF_CQ + MLA_Q_RANK], qnw_ref[0])
    qc = jnp.dot(cq.astype(BF16), w_uq_ref[...], preferred_element_type=F32)
    if rope:
        cm, sm = cm_ref[...], sm_ref[...]
        qc = jnp.concatenate(
            [_rope_slab(qc[:, hd * MLA_HEAD_PAD:(hd + 1) * MLA_HEAD_PAD], cm, sm) for hd in range(MLA_HEADS)],
            axis=1)

    ckv = _rms_norm(proj[:, OFF_CKV:OFF_CKV + MLA_KV_RANK], kvnw_ref[0])
    kpe_slab = proj[:, OFF_KPE:OFF_KPE + LANES]
    if rope:
        kpe_slab = _rope_slab(kpe_slab, cp_ref[...], sp_ref[...])
    k_cat, v_m = _expand_kv(ckv, kpe_slab, w_kv_ref)
    kc_ref[...] = k_cat

    q_t, v_t, qc_t, vm_t = q.T.astype(BF16), v.T.astype(BF16), qc.T.astype(BF16), v_m.T.astype(BF16)
    for j in range(max(n_seq, 1)):
        cols = slice(j * seq, (j + 1) * seq) if n_seq else slice(None)
        for dst, val in ((qdt_ref, q_t), (vdt_ref, v_t), (qmt_ref, qc_t), (vmt_ref, vm_t)):
            (dst.at[j] if n_seq else dst)[...] = val[:, cols]

    if with_state:
        fills = [()] if alias_state else [(ll,) for ll in range(stc_ref.shape[1])]
        for j in range(n_seq):
            rows = slice(j * seq, (j + 1) * seq)
            for ll in fills:
                stc_ref[(j,) + ll] = ckv[rows]
                stp_ref[(j,) + ll] = kpe_slab[rows, :MLA_ROPE]
                for hm in range(2 * DIFF_HEADS):
                    stk_ref[(j,) + ll + (hm // 2, hm % 2)] = k[rows, hm * DIFF_DK:(hm + 1) * DIFF_DK]
                for hd in range(DIFF_HEADS):
                    stv_ref[(j,) + ll + (hd,)] = v[rows, hd * DIFF_DV:(hd + 1) * DIFF_DV]


def _inproj_call(x2d, mod3, wts, l, *, n_batch, seq, tm, mod_row, rope_tabs, n_layers, states):
    rows = n_batch * seq
    n_tiles = rows // tm
    rope = rope_tabs is not None
    with_state = states is not None
    alias_state = bool(states)
    n_seq = tm // seq
    t_per_seq = max(seq // tm, 1)

    def mod_spec(j):
        return pl.BlockSpec((1, 1, D_MODEL), lambda i: ((l * COND_ROWS + mod_row(i)) * N_MOD + j, 0, 0))

    def row_spec(width):
        return pl.BlockSpec((tm, width), lambda i: (i, 0))

    def feat_spec(width):
        if n_seq:
            return pl.BlockSpec((n_seq, width, seq), lambda i: (i, 0, 0))
        return pl.BlockSpec((None, width, tm), lambda i: (i // t_per_seq, 0, i % t_per_seq))

    in_specs = [
        row_spec(D_MODEL), mod_spec(1), mod_spec(0),
        _layer_spec((D_MODEL, IN_COLS_PAD), l),
        _layer_spec((1, MLA_Q_RANK), l),
        _layer_spec((MLA_Q_RANK, MLA_QK_PAD), l),
        _layer_spec((1, MLA_KV_RANK), l),
        _layer_spec((MLA_KV_RANK, MLA_QK_PAD + MLA_WIDTH), l),
    ]
    args = [x2d, mod3, mod3, wts["w_in"], wts["q_norm_w"], wts["w_uq"], wts["kv_norm_w"], wts["w_kv"]]
    if rope:
        in_specs += [pl.BlockSpec((tm, LANES), lambda i: (i % t_per_seq, 0))] * 6
        args += list(rope_tabs)

    out_shape = [
        jax.ShapeDtypeStruct((rows, CONV_W), F32),
        jax.ShapeDtypeStruct((rows, CONV_W), F32),
        jax.ShapeDtypeStruct((n_batch, DIFF_QK_COLS, seq), BF16),
        jax.ShapeDtypeStruct((rows, DIFF_QK_COLS), BF16),
        jax.ShapeDtypeStruct((n_batch, DIFF_WIDTH, seq), BF16),
        jax.ShapeDtypeStruct((n_batch, MLA_QK_PAD, seq), BF16),
        jax.ShapeDtypeStruct((rows, MLA_QK_PAD), BF16),
        jax.ShapeDtypeStruct((n_batch, MLA_WIDTH, seq), BF16),
    ]
    out_specs = [row_spec(CONV_W), row_spec(CONV_W), feat_spec(DIFF_QK_COLS), row_spec(DIFF_QK_COLS),
                 feat_spec(DIFF_WIDTH), feat_spec(MLA_QK_PAD), row_spec(MLA_QK_PAD), feat_spec(MLA_WIDTH)]
    aliases = {}
    if with_state:
        assert n_seq >= 1
        n_out = len(out_shape)
        out_shape += [
            jax.ShapeDtypeStruct((n_batch, n_layers, DIFF_HEADS, 2, seq, DIFF_DK), F32),
            jax.ShapeDtypeStruct((n_batch, n_layers, DIFF_HEADS, seq, DIFF_DV), F32),
            jax.ShapeDtypeStruct((n_batch, n_layers, seq, MLA_KV_RANK), F32),
            jax.ShapeDtypeStruct((n_batch, n_layers, seq, MLA_ROPE), F32),
        ]
        lyr, l_idx = (None, l) if alias_state else (n_layers, 0)
        out_specs += [
            pl.BlockSpec((n_seq, lyr, DIFF_HEADS, 2, seq, DIFF_DK), lambda i: (i, l_idx, 0, 0, 0, 0)),
            pl.BlockSpec((n_seq, lyr, DIFF_HEADS, seq, DIFF_DV), lambda i: (i, l_idx, 0, 0, 0)),
            pl.BlockSpec((n_seq, lyr, seq, MLA_KV_RANK), lambda i: (i, l_idx, 0, 0)),
            pl.BlockSpec((n_seq, lyr, seq, MLA_ROPE), lambda i: (i, l_idx, 0, 0)),
        ]
        if alias_state:
            for j, st in enumerate(states):
                aliases[len(args)] = n_out + j
                in_specs.append(pl.BlockSpec(memory_space=pl.ANY))
                args.append(st)

    kern = functools.partial(_inproj_kernel, rope=rope, n_seq=n_seq, seq=seq, with_state=with_state,
                             alias_state=alias_state)
    return pl.pallas_call(
        kern, out_shape=tuple(out_shape), grid=(n_tiles,),
        in_specs=in_specs, out_specs=tuple(out_specs), input_output_aliases=aliases,
        compiler_params=_cparams(1), name=f"inproj_l{l}_{'s' if rope else 'p'}",
    )(*args)


def _tree(op, xs):
    xs = list(xs)
    while len(xs) > 1:
        xs = [op(xs[i], xs[i + 1]) if i + 1 < len(xs) else xs[i] for i in range(0, len(xs), 2)]
    return xs[0]


def _slabs(x):
    return [x[j * SUBLANES:(j + 1) * SUBLANES] for j in range(x.shape[0] // SUBLANES)]


class _Head:
    def __init__(self, k_refs, rhs_fn, vt_refs, scale_log2e, slot):
        self.k_refs, self.rhs_fn, self.vt_refs, self.scale, self.slot = k_refs, rhs_fn, vt_refs, scale_log2e, slot


def _scores(head, s_scr):
    rhs = head.rhs_fn()
    m8, off = None, 0
    for k_ref in head.k_refs:
        n = k_ref.shape[0]
        s = jnp.dot(k_ref[...], rhs, preferred_element_type=F32)
        s_scr[head.slot][off:off + n, :] = s
        blk = _tree(jnp.maximum, _slabs(s))
        m8 = blk if m8 is None else jnp.maximum(m8, blk)
        off += n
    return jnp.max(m8, axis=0, keepdims=True)


def _probs(head, m, s_scr, p_scr):
    for c0 in range(0, s_scr[head.slot].shape[0], KEY_CHUNK):
        p = jnp.exp2((s_scr[head.slot][c0:c0 + KEY_CHUNK, :] - m) * head.scale)
        p_scr[head.slot][c0:c0 + KEY_CHUNK, :] = p.astype(BF16)


def _weighted_values(head, p_scr):
    o, off = None, 0
    for vt_ref in head.vt_refs:
        n = vt_ref.shape[1]
        v_ext = jnp.concatenate([vt_ref[...], jnp.ones((BF16_ROWS, n), BF16)], axis=0)
        pv = jnp.dot(v_ext, p_scr[head.slot][off:off + n, :], preferred_element_type=F32)
        o = pv if o is None else o + pv
        off += n
    dv = o.shape[0] - BF16_ROWS
    return o[:dv] * (1.0 / o[dv:dv + 1])


def _run_heads(heads, s_scr, p_scr, finish):
    m = _scores(heads[0], s_scr)
    for t, head in enumerate(heads):
        m_next = _scores(heads[t + 1], s_scr) if t + 1 < len(heads) else None
        _probs(head, m, s_scr, p_scr)
        finish(t, _weighted_values(head, p_scr))
        m = m_next


def _attn_kernel(*refs, lam_init, has_cache):
    it = iter(refs)
    lam_ref, dnw_ref = next(it), next(it)
    qdt_ref, kd_ref, vdt_ref, qmt_ref, kc_ref, vmt_ref = (next(it) for _ in range(6))
    kds, vdts, kcs, vmts = [kd_ref], [vdt_ref], [kc_ref], [vmt_ref]
    if has_cache:
        kds.append(next(it)); vdts.append(next(it)); kcs.append(next(it)); vmts.append(next(it))
    yb_ref, yc_ref = next(it), next(it)
    s_scr, p_scr = [next(it), next(it)], [next(it), next(it)]

    lv = lam_ref[0]
    lam = (jnp.exp(jnp.sum(lv[0:1] * lv[1:2], axis=-1, keepdims=True))
           - jnp.exp(jnp.sum(lv[2:3] * lv[3:4], axis=-1, keepdims=True)) + lam_init)

    def diff_rhs(hm):
        def fn():
            q_dt = qdt_ref[...]
            feat = lax.broadcasted_iota(jnp.int32, q_dt.shape, 0)
            return jnp.where((feat >= hm * DIFF_DK) & (feat < (hm + 1) * DIFF_DK), q_dt, jnp.zeros_like(q_dt))
        return fn

    def mla_rhs(feats):
        return lambda: qmt_ref[feats, :]

    heads = []
    for hm in range(2 * DIFF_HEADS):
        rows = slice((hm // 2) * DIFF_DV, (hm // 2 + 1) * DIFF_DV)
        heads.append(_Head(kds, diff_rhs(hm), [vt.at[rows, :] for vt in vdts], DIFF_SCALE * LOG2E, len(heads) % 2))
    for hd in range(MLA_HEADS):
        feats = slice(hd * MLA_HEAD_PAD, (hd + 1) * MLA_HEAD_PAD)
        rows = slice(hd * MLA_V, (hd + 1) * MLA_V)
        heads.append(_Head([kc.at[:, feats] for kc in kcs], mla_rhs(feats), [vt.at[rows, :] for vt in vmts],
                           MLA_SCALE * LOG2E, len(heads) % 2))

    outs = []

    def finish(t, o):
        outs.append(o)
        if t == 2 * DIFF_HEADS - 1:
            y_heads = []
            for hd in range(DIFF_HEADS):
                d = outs[2 * hd] - lam * outs[2 * hd + 1]
                ms = jnp.mean(d * d, axis=0, keepdims=True)
                y_heads.append(d * lax.rsqrt(ms + 1e-6))
            y_b = jnp.concatenate(y_heads, axis=0).T
            yb_ref[...] = (y_b * dnw_ref[0] * (1.0 - lam_init)).astype(BF16)
        if t == len(heads) - 1:
            yc_ref[...] = jnp.concatenate(outs[2 * DIFF_HEADS:], axis=0).T.astype(BF16)

    _run_heads(heads, s_scr, p_scr, finish)


def _attn_call(lamv, dnw, qdt, kd, vdt, qmt, kc, vmt, cache, l, *, n_batch, seq, tq, lam_init):
    t_per_seq = seq // tq
    has_cache = cache is not None

    def q_spec(width):
        return pl.BlockSpec((None, width, tq), lambda b, t: (b, 0, t))

    def out_spec(width):
        return pl.BlockSpec((tq, width), lambda b, t: (b * t_per_seq + t, 0))

    in_specs = [
        pl.BlockSpec((1, 4, DIFF_DK), lambda b, t: (l, 0, 0)),
        pl.BlockSpec((1, 1, DIFF_WIDTH), lambda b, t: (l, 0, 0)),
        q_spec(DIFF_QK_COLS),
        pl.BlockSpec((seq, DIFF_QK_COLS), lambda b, t: (b, 0)),
        pl.BlockSpec((None, DIFF_WIDTH, seq), lambda b, t: (b, 0, 0)),
        q_spec(MLA_QK_PAD),
        pl.BlockSpec((seq, MLA_QK_PAD), lambda b, t: (b, 0)),
        pl.BlockSpec((None, MLA_WIDTH, seq), lambda b, t: (b, 0, 0)),
    ]
    args = [lamv, dnw, qdt, kd, vdt, qmt, kc, vmt]
    past = 0
    if has_cache:
        past = cache[0].shape[2]
        in_specs += [
            pl.BlockSpec((None, None, past, DIFF_QK_COLS), lambda b, t: (b, l, 0, 0)),
            pl.BlockSpec((None, None, DIFF_WIDTH, past), lambda b, t: (b, l, 0, 0)),
            pl.BlockSpec((None, None, past, MLA_QK_PAD), lambda b, t: (b, l, 0, 0)),
            pl.BlockSpec((None, None, MLA_WIDTH, past), lambda b, t: (b, l, 0, 0)),
        ]
        args += list(cache)
    rows = n_batch * seq
    nk = seq + past
    return pl.pallas_call(
        functools.partial(_attn_kernel, lam_init=lam_init, has_cache=has_cache),
        out_shape=(jax.ShapeDtypeStruct((rows, DIFF_WIDTH), BF16),
                   jax.ShapeDtypeStruct((rows, MLA_WIDTH), BF16)),
        grid=(n_batch, t_per_seq),
        in_specs=in_specs,
        out_specs=(out_spec(DIFF_WIDTH), out_spec(MLA_WIDTH)),
        scratch_shapes=[pltpu.VMEM((nk, tq), F32)] * 2 + [pltpu.VMEM((nk, tq), BF16)] * 2,
        compiler_params=_cparams(2), name=f"attn_l{l}_{'s' if has_cache else 'p'}",
    )(*args)


def _tail_kernel(x_ref, u_ref, up_ref, un_ref, ab_ref, yb_ref, yc_ref, cw_ref, g1_ref, sh2_ref, sc2_ref, g2_ref,
                 ln1g_ref, ln1b_ref, ln2g_ref, ln2b_ref, w_out_ref, w1_ref, w3_ref, w2_ref, o_ref, *, seq):
    tm = x_ref.shape[0]
    u = u_ref[...]
    row = lax.broadcasted_iota(jnp.int32, (tm, 1), 0)
    pos = (row + pl.program_id(0) * tm) % seq
    u_prev = jnp.where(row == 0, up_ref[SUBLANES - 1:SUBLANES, :], pltpu.roll(u, 1, axis=0))
    u_prev = jnp.where(pos == 0, 0.0, u_prev)
    u_next = jnp.where(row == tm - 1, un_ref[0:1, :], pltpu.roll(u, tm - 1, axis=0))
    u_next = jnp.where(pos == seq - 1, 0.0, u_next)
    cw = cw_ref[0]
    y_a = ab_ref[...] * (u_prev * cw[0:1] + u * cw[1:2] + u_next * cw[2:3])

    y_cat = jnp.concatenate([y_a.astype(BF16), yb_ref[...], yc_ref[...]], axis=1)
    y = jnp.dot(y_cat, w_out_ref[...], preferred_element_type=F32)
    x1 = _layer_norm(DEEPNORM_ALPHA * x_ref[...] + g1_ref[0] * y, ln1g_ref[0], ln1b_ref[0])

    hf = (x1 * (1.0 + sc2_ref[0]) + sh2_ref[0]).astype(BF16)
    f = None
    for c0, c1 in FF_CHUNKS:
        g = jnp.dot(hf, w1_ref[:, c0:c1], preferred_element_type=F32)
        a = (g * _sigmoid(g) * jnp.dot(hf, w3_ref[:, c0:c1], preferred_element_type=F32)).astype(BF16)
        t = jnp.dot(a, w2_ref[c0:c1, :], preferred_element_type=F32)
        f = t if f is None else f + t
    o_ref[...] = _layer_norm(DEEPNORM_ALPHA * x1 + g2_ref[0] * f, ln2g_ref[0], ln2b_ref[0])


def _tail_call(x2d, u, ab, yb, yc, mod3, wts, l, *, seq, tm, mod_row):
    rows = x2d.shape[0]
    n_tiles = rows // tm
    n_halo = rows // SUBLANES

    def mod_spec(j):
        return pl.BlockSpec((1, 1, D_MODEL), lambda i: ((l * COND_ROWS + mod_row(i)) * N_MOD + j, 0, 0))

    def row_spec(width):
        return pl.BlockSpec((tm, width), lambda i: (i, 0))

    in_specs = [
        row_spec(D_MODEL), row_spec(CONV_W),
        pl.BlockSpec((SUBLANES, CONV_W), lambda i: (jnp.maximum(i * (tm // SUBLANES) - 1, 0), 0)),
        pl.BlockSpec((SUBLANES, CONV_W), lambda i: (jnp.minimum((i + 1) * (tm // SUBLANES), n_halo - 1), 0)),
        row_spec(CONV_W), row_spec(DIFF_WIDTH), row_spec(MLA_WIDTH),
        pl.BlockSpec((1, 3, CONV_W), lambda i: (l, 0, 0)),
        mod_spec(2), mod_spec(3), mod_spec(4), mod_spec(5),
        _layer_spec((1, D_MODEL), l), _layer_spec((1, D_MODEL), l),
        _layer_spec((1, D_MODEL), l), _layer_spec((1, D_MODEL), l),
        _layer_spec((D_MODEL, D_MODEL), l),
        _layer_spec((D_MODEL, D_FF), l), _layer_spec((D_MODEL, D_FF), l), _layer_spec((D_FF, D_MODEL), l),
    ]
    return pl.pallas_call(
        functools.partial(_tail_kernel, seq=seq),
        out_shape=jax.ShapeDtypeStruct((rows, D_MODEL), F32),
        grid=(n_tiles,),
        in_specs=in_specs,
        out_specs=row_spec(D_MODEL),
        compiler_params=_cparams(1), name=f"tail_l{l}",
    )(x2d, u, u, u, ab, yb, yc, wts["conv_w"], mod3, mod3, mod3, mod3,
      wts["ln1_g"], wts["ln1_b"], wts["ln2_g"], wts["ln2_b"],
      wts["w_out"], wts["w_ff1"], wts["w_ff3"], wts["w_ff2"])


def _rope_tables(n_tokens):
    rows = n_tokens // GRID_W
    row = np.repeat(np.arange(rows), GRID_W).astype(np.float32)
    col = np.tile(np.arange(GRID_W), rows).astype(np.float32)
    half = MLA_ROPE // 4
    freqs = (np.float32(ROPE_BASE) ** (-np.arange(half, dtype=np.float32) / np.float32(half))).astype(np.float32)
    ang_r = row[:, None] * freqs[None, :]
    ang_c = col[:, None] * freqs[None, :]
    cos32 = np.concatenate([np.cos(ang_r), np.cos(ang_r), np.cos(ang_c), np.cos(ang_c)], axis=1)
    sin32 = np.concatenate([-np.sin(ang_r), np.sin(ang_r), -np.sin(ang_c), np.sin(ang_c)], axis=1)
    one = np.ones((n_tokens, 1), np.float32)
    zero = np.zeros((n_tokens, 1), np.float32)
    cos_d, sin_d = np.tile(cos32, (1, 4)), np.tile(sin32, (1, 4))
    cos_m = np.concatenate([np.tile(one, (1, MLA_NOPE)), cos32, np.tile(one, (1, 32))], axis=1)
    sin_m = np.concatenate([np.tile(zero, (1, MLA_NOPE)), sin32, np.tile(zero, (1, 32))], axis=1)
    cos_p = np.concatenate([cos32, np.tile(one, (1, 96))], axis=1)
    sin_p = np.concatenate([sin32, np.tile(zero, (1, 96))], axis=1)
    return tuple(jnp.asarray(t.astype(np.float32)) for t in (cos_d, sin_d, cos_m, sin_m, cos_p, sin_p))


def _prep_weights(w_in, w_uq, w_ukv, w_out, w_ff1, w_ff3, w_ff2, conv_w, q_norm_w, kv_norm_w,
                  ln1_g, ln1_b, ln2_g, ln2_b):
    n_l = w_in.shape[0]
    w_in_p = jnp.pad(w_in, ((0, 0), (0, 0), (0, IN_COLS_PAD - IN_COLS))).astype(BF16)
    w_uq_p = jnp.pad(w_uq.reshape(n_l, MLA_Q_RANK, MLA_HEADS, MLA_QK),
                     ((0, 0), (0, 0), (0, 0), (0, MLA_HEAD_PAD - MLA_QK))).reshape(n_l, MLA_Q_RANK, MLA_QK_PAD)
    w_ukv4 = w_ukv.reshape(n_l, MLA_KV_RANK, MLA_HEADS, MLA_NOPE + MLA_V)
    w_uk_p = jnp.pad(w_ukv4[..., :MLA_NOPE], ((0, 0), (0, 0), (0, 0), (0, MLA_HEAD_PAD - MLA_NOPE)))
    w_kv = jnp.concatenate([w_uk_p.reshape(n_l, MLA_KV_RANK, MLA_QK_PAD),
                            w_ukv4[..., MLA_NOPE:].reshape(n_l, MLA_KV_RANK, MLA_WIDTH)], axis=-1)
    vec = lambda a: a.reshape(n_l, 1, a.shape[-1])
    return {
        "w_in": w_in_p, "w_uq": w_uq_p.astype(BF16), "w_kv": w_kv.astype(BF16),
        "w_out": w_out.astype(BF16), "w_ff1": w_ff1.astype(BF16), "w_ff3": w_ff3.astype(BF16),
        "w_ff2": w_ff2.astype(BF16), "conv_w": conv_w,
        "q_norm_w": vec(q_norm_w), "kv_norm_w": vec(kv_norm_w),
        "ln1_g": vec(ln1_g), "ln1_b": vec(ln1_b), "ln2_g": vec(ln2_g), "ln2_b": vec(ln2_b),
    }


def kernel(x_prompt, x_sample, cache_diff_k, cache_diff_v, cache_mla_ckv, cache_mla_kpe, c, c_ctx,
           w_ada, b_ada, w_in, conv_w, lam_q1, lam_k1, lam_q2, lam_k2, diff_norm_w, q_norm_w, w_uq,
           kv_norm_w, w_ukv, w_out, ln1_g, ln1_b, w_ff1, w_ff3, w_ff2, ln2_g, ln2_b):
    n_l = w_in.shape[0]
    pb, ps, _ = x_prompt.shape
    sb, ss, _ = x_sample.shape
    past = cache_mla_ckv.shape[2]

    wts = _prep_weights(w_in, w_uq, w_ukv, w_out, w_ff1, w_ff3, w_ff2, conv_w, q_norm_w, kv_norm_w,
                        ln1_g, ln1_b, ln2_g, ln2_b)
    lamv = jnp.stack([lam_q1, lam_k1, lam_q2, lam_k2], axis=1)
    dnw = jnp.tile(diff_norm_w, (1, DIFF_HEADS)).reshape(n_l, 1, DIFF_WIDTH)

    cond = jnp.concatenate([c_ctx[None, :], c, jnp.zeros((COND_ROWS - 1 - sb, D_MODEL), F32)], axis=0)
    mod3 = _ada_call(cond, w_ada, b_ada).reshape(n_l * COND_ROWS * N_MOD, 1, D_MODEL)

    ck = jnp.transpose(cache_diff_k, (0, 1, 4, 2, 3, 5)).reshape(sb, n_l, past, DIFF_QK_COLS).astype(BF16)
    cv_t = jnp.swapaxes(cache_diff_v, -1, -2).reshape(sb, n_l, DIFF_WIDTH, past).astype(BF16)
    kpe_slab = jnp.pad(cache_mla_kpe, ((0, 0), (0, 0), (0, 0), (0, LANES - MLA_ROPE)))
    ckc, cvm_t = _cache_kv_call(cache_mla_ckv, kpe_slab, wts["w_kv"])

    rope_tabs = _rope_tables(ss)
    tm_p, tm_s, tq_s = 512, 512, 512

    xp = x_prompt.reshape(pb * ps, D_MODEL)
    xs = x_sample.reshape(sb * ss, D_MODEL)
    states = ()
    for l in range(n_l):
        lam_init = 0.8 - 0.6 * math.exp(-0.3 * l)

        row_p = lambda i: 0
        outs = _inproj_call(xp, mod3, wts, l, n_batch=pb, seq=ps, tm=tm_p, mod_row=row_p, rope_tabs=None,
                            n_layers=n_l, states=states)
        u, ab, qdt, kd, vdt, qmt, kc, vmt = outs[:8]
        states = tuple(outs[8:])
        yb, yc = _attn_call(lamv, dnw, qdt, kd, vdt, qmt, kc, vmt, None, l,
                            n_batch=pb, seq=ps, tq=ps, lam_init=lam_init)
        xp = _tail_call(xp, u, ab, yb, yc, mod3, wts, l, seq=ps, tm=tm_p, mod_row=row_p)

        row_s = lambda i: 1 + i // (ss // tm_s)
        u, ab, qdt, kd, vdt, qmt, kc, vmt = _inproj_call(
            xs, mod3, wts, l, n_batch=sb, seq=ss, tm=tm_s, mod_row=row_s, rope_tabs=rope_tabs,
            n_layers=n_l, states=None)
        yb, yc = _attn_call(lamv, dnw, qdt, kd, vdt, qmt, kc, vmt, (ck, cv_t, ckc, cvm_t), l,
                            n_batch=sb, seq=ss, tq=tq_s, lam_init=lam_init)
        xs = _tail_call(xs, u, ab, yb, yc, mod3, wts, l, seq=ss, tm=tm_s, mod_row=row_s)

    return (xp.reshape(pb, ps, D_MODEL), xs.reshape(sb, ss, D_MODEL)) + states
```

```python
import functools
import math

import numpy as np
import jax
import jax.numpy as jnp
from jax import lax
from jax.experimental import pallas as pl
from jax.experimental.pallas import tpu as pltpu

F32 = jnp.float32
BF16 = jnp.bfloat16

D_MODEL = 1024
DEPTH = 2
GRID_W = 64
ROPE_BASE = 10000.0
CONV_W = 256
DIFF_HEADS = 4
DIFF_DK = 32
DIFF_DV = 64
DIFF_WIDTH = DIFF_HEADS * DIFF_DV
DIFF_QK_COLS = DIFF_HEADS * 2 * DIFF_DK
DIFF_SCALE = DIFF_DK ** -0.5
MLA_HEADS = 8
MLA_Q_RANK = 384
MLA_KV_RANK = 256
MLA_NOPE = 64
MLA_ROPE = 32
MLA_V = 64
MLA_QK = MLA_NOPE + MLA_ROPE
MLA_WIDTH = MLA_HEADS * MLA_V
MLA_SCALE = MLA_QK ** -0.5
IN_COLS = 3 * CONV_W + 2 * DIFF_QK_COLS + DIFF_WIDTH + MLA_Q_RANK + MLA_KV_RANK + MLA_ROPE
D_FF = 2816
DEEPNORM_ALPHA = (2 * DEPTH) ** 0.25
LOG2E = 1.4426950408889634

LANES = 128
SUBLANES = 8
BF16_ROWS = 16
MXU_DIM = 256
IN_COLS_PAD = 2304
MLA_HEAD_PAD = LANES
MLA_QK_PAD = MLA_HEADS * MLA_HEAD_PAD
N_MOD = 6
COND_ROWS = 8
FF_CHUNKS = ((0, 1536), (1536, 2816))
KEY_CHUNK = 64
LOOKAHEAD = 3
VMEM_LIMIT = 56 * 2 ** 20

OFF_AX, OFF_AB, OFF_AC = 0, 256, 512
OFF_DQ, OFF_DK, OFF_DV = 768, 1024, 1280
OFF_CQ, OFF_CKV, OFF_KPE = 1536, 1920, 2176


def _cparams(n_grid):
    return pltpu.CompilerParams(dimension_semantics=("arbitrary",) * n_grid,
                                vmem_limit_bytes=VMEM_LIMIT)


def _layer_spec(tail, l):
    nd = len(tail)
    return pl.BlockSpec((None,) + tuple(tail), lambda *_: (l,) + (0,) * nd,
                        pipeline_mode=pl.Buffered(1))


def _sigmoid(x):
    return 1.0 / (1.0 + jnp.exp(-x))


def _layer_norm(x, g, b):
    mu = jnp.mean(x, axis=-1, keepdims=True)
    xc = x - mu
    var = jnp.mean(xc * xc, axis=-1, keepdims=True)
    return xc * lax.rsqrt(var + 1e-5) * g + b


def _rms_norm(x, w):
    ms = jnp.mean(x * x, axis=-1, keepdims=True)
    return x * lax.rsqrt(ms + 1e-6) * w


def _rope_slab(x, cos, sin_signed):
    lane = lax.broadcasted_iota(jnp.int32, x.shape, 1)
    lo = (lane % 16) < 8
    partner = jnp.where(lo, pltpu.roll(x, LANES - 8, axis=1), pltpu.roll(x, 8, axis=1))
    return x * cos + partner * sin_signed


def _ada_kernel(cond_ref, w_ref, b_ref, o_ref):
    c = cond_ref[...]
    s = (c * _sigmoid(c)).astype(BF16)
    o_ref[0] = jnp.dot(s, w_ref[0].astype(BF16), preferred_element_type=F32) + b_ref[0]


def _ada_call(cond, w_ada, b_ada):
    n_l = w_ada.shape[0]
    tn = D_MODEL
    return pl.pallas_call(
        _ada_kernel,
        out_shape=jax.ShapeDtypeStruct((n_l, COND_ROWS, N_MOD * D_MODEL), F32),
        grid=(n_l, N_MOD * D_MODEL // tn),
        in_specs=[
            pl.BlockSpec((COND_ROWS, D_MODEL), lambda l, j: (0, 0)),
            pl.BlockSpec((1, D_MODEL, tn), lambda l, j: (l, 0, j)),
            pl.BlockSpec((1, 1, tn), lambda l, j: (l, 0, j)),
        ],
        out_specs=pl.BlockSpec((1, COND_ROWS, tn), lambda l, j: (l, 0, j)),
        compiler_params=_cparams(2),
        name="ada_mod",
    )(cond, w_ada, b_ada.reshape(n_l, 1, N_MOD * D_MODEL))


def _expand_kv(ckv, kpe_slab, w_kv_ref):
    kv = jnp.dot(ckv.astype(BF16), w_kv_ref[...], preferred_element_type=F32)
    kpe_shift = pltpu.roll(kpe_slab, MLA_NOPE, axis=1)
    k_cat = jnp.concatenate(
        [(kv[:, h * MLA_HEAD_PAD:(h + 1) * MLA_HEAD_PAD] + kpe_shift).astype(BF16) for h in range(MLA_HEADS)],
        axis=1)
    return k_cat, kv[:, MLA_QK_PAD:MLA_QK_PAD + MLA_WIDTH]


def _cache_kv_kernel(ckv_ref, kpe_ref, w_kv_ref, kc_ref, vmt_ref):
    k_cat, v_m = _expand_kv(ckv_ref[...], kpe_ref[...], w_kv_ref)
    kc_ref[...] = k_cat
    vmt_ref[...] = v_m.T.astype(BF16)


def _cache_kv_call(cache_ckv, cache_kpe_slab, w_kv):
    n_b, n_l, past, _ = cache_ckv.shape
    return pl.pallas_call(
        _cache_kv_kernel,
        out_shape=(jax.ShapeDtypeStruct((n_b, n_l, past, MLA_QK_PAD), BF16),
                   jax.ShapeDtypeStruct((n_b, n_l, MLA_WIDTH, past), BF16)),
        grid=(n_b, n_l),
        in_specs=[
            pl.BlockSpec((None, None, past, MLA_KV_RANK), lambda b, l: (b, l, 0, 0)),
            pl.BlockSpec((None, None, past, LANES), lambda b, l: (b, l, 0, 0)),
            pl.BlockSpec((None, MLA_KV_RANK, MLA_QK_PAD + MLA_WIDTH), lambda b, l: (l, 0, 0)),
        ],
        out_specs=(pl.BlockSpec((None, None, past, MLA_QK_PAD), lambda b, l: (b, l, 0, 0)),
                   pl.BlockSpec((None, None, MLA_WIDTH, past), lambda b, l: (b, l, 0, 0))),
        compiler_params=_cparams(2),
        name="cache_kv",
    )(cache_ckv, cache_kpe_slab, w_kv)


N_STATE = 4


def _inproj_kernel(*refs, rope, n_seq, seq, with_state, alias_state):
    it = iter(refs)
    x_ref, sc_ref, sh_ref = next(it), next(it), next(it)
    w_in_ref, qnw_ref, w_uq_ref, kvnw_ref, w_kv_ref = next(it), next(it), next(it), next(it), next(it)
    if rope:
        cd_ref, sd_ref, cm_ref, sm_ref, cp_ref, sp_ref = (next(it) for _ in range(6))
    if alias_state:
        for _ in range(N_STATE):
            next(it)
    u_ref, ab_ref, qdt_ref, kd_ref, vdt_ref, qmt_ref, kc_ref, vmt_ref = (next(it) for _ in range(8))
    if with_state:
        stk_ref, stv_ref, stc_ref, stp_ref = (next(it) for _ in range(N_STATE))

    h = (x_ref[...] * (1.0 + sc_ref[0]) + sh_ref[0]).astype(BF16)
    proj = jnp.dot(h, w_in_ref[...], preferred_element_type=F32)

    u_ref[...] = proj[:, OFF_AC:OFF_AC + CONV_W] * proj[:, OFF_AX:OFF_AX + CONV_W]
    ab_ref[...] = proj[:, OFF_AB:OFF_AB + CONV_W]

    q = proj[:, OFF_DQ:OFF_DQ + DIFF_QK_COLS]
    k = proj[:, OFF_DK:OFF_DK + DIFF_QK_COLS]
    v = proj[:, OFF_DV:OFF_DV + DIFF_WIDTH]
    if rope:
        cd, sd = cd_ref[...], sd_ref[...]
        q = jnp.concatenate([_rope_slab(q[:, j * LANES:(j + 1) * LANES], cd, sd) for j in range(2)], axis=1)
        k = jnp.concatenate([_rope_slab(k[:, j * LANES:(j + 1) * LANES], cd, sd) for j in range(2)], axis=1)
    kd_ref[...] = k.astype(BF16)

    cq = _rms_norm(proj[:, OFF_CQ:OFF_CQ + MLA_Q_RANK], qnw_ref[0])
    qc = jnp.dot(cq.astype(BF16), w_uq_ref[...], preferred_element_type=F32)
    if rope:
        cm, sm = cm_ref[...], sm_ref[...]
        qc = jnp.concatenate(
            [_rope_slab(qc[:, hd * MLA_HEAD_PAD:(hd + 1) * MLA_HEAD_PAD], cm, sm) for hd in range(MLA_HEADS)],
            axis=1)

    ckv = _rms_norm(proj[:, OFF_CKV:OFF_CKV + MLA_KV_RANK], kvnw_ref[0])
    kpe_slab = proj[:, OFF_KPE:OFF_KPE + LANES]
    if rope:
        kpe_slab = _rope_slab(kpe_slab, cp_ref[...], sp_ref[...])
    k_cat, v_m = _expand_kv(ckv, kpe_slab, w_kv_ref)
    kc_ref[...] = k_cat

    q_t, v_t, qc_t, vm_t = q.T.astype(BF16), v.T.astype(BF16), qc.T.astype(BF16), v_m.T.astype(BF16)
    for j in range(max(n_seq, 1)):
        cols = slice(j * seq, (j + 1) * seq) if n_seq else slice(None)
        for dst, val in ((qdt_ref, q_t), (vdt_ref, v_t), (qmt_ref, qc_t), (vmt_ref, vm_t)):
            (dst.at[j] if n_seq else dst)[...] = val[:, cols]

    if with_state:
        fills = [()] if alias_state else [(ll,) for ll in range(stc_ref.shape[1])]
        for j in range(n_seq):
            rows = slice(j * seq, (j + 1) * seq)
            for ll in fills:
                stc_ref[(j,) + ll] = ckv[rows]
                stp_ref[(j,) + ll] = kpe_slab[rows, :MLA_ROPE]
                for hm in range(2 * DIFF_HEADS):
                    stk_ref[(j,) + ll + (hm // 2, hm % 2)] = k[rows, hm * DIFF_DK:(hm + 1) * DIFF_DK]
                for hd in range(DIFF_HEADS):
                    stv_ref[(j,) + ll + (hd,)] = v[rows, hd * DIFF_DV:(hd + 1) * DIFF_DV]


def _inproj_call(x2d, mod3, wts, l, *, n_batch, seq, tm, mod_row, rope_tabs, n_layers, states):
    rows = n_batch * seq
    n_tiles = rows // tm
    rope = rope_tabs is not None
    with_state = states is not None
    alias_state = bool(states)
    n_seq = tm // seq
    t_per_seq = max(seq // tm, 1)

    def mod_spec(j):
        return pl.BlockSpec((1, 1, D_MODEL), lambda i: ((l * COND_ROWS + mod_row(i)) * N_MOD + j, 0, 0))

    def row_spec(width):
        return pl.BlockSpec((tm, width), lambda i: (i, 0))

    def feat_spec(width):
        if n_seq:
            return pl.BlockSpec((n_seq, width, seq), lambda i: (i, 0, 0))
        return pl.BlockSpec((None, width, tm), lambda i: (i // t_per_seq, 0, i % t_per_seq))

    in_specs = [
        row_spec(D_MODEL), mod_spec(1), mod_spec(0),
        _layer_spec((D_MODEL, IN_COLS_PAD), l),
        _layer_spec((1, MLA_Q_RANK), l),
        _layer_spec((MLA_Q_RANK, MLA_QK_PAD), l),
        _layer_spec((1, MLA_KV_RANK), l),
        _layer_spec((MLA_KV_RANK, MLA_QK_PAD + MLA_WIDTH), l),
    ]
    args = [x2d, mod3, mod3, wts["w_in"], wts["q_norm_w"], wts["w_uq"], wts["kv_norm_w"], wts["w_kv"]]
    if rope:
        in_specs += [pl.BlockSpec((tm, LANES), lambda i: (i % t_per_seq, 0))] * 6
        args += list(rope_tabs)

    out_shape = [
        jax.ShapeDtypeStruct((rows, CONV_W), F32),
        jax.ShapeDtypeStruct((rows, CONV_W), F32),
        jax.ShapeDtypeStruct((n_batch, DIFF_QK_COLS, seq), BF16),
        jax.ShapeDtypeStruct((rows, DIFF_QK_COLS), BF16),
        jax.ShapeDtypeStruct((n_batch, DIFF_WIDTH, seq), BF16),
        jax.ShapeDtypeStruct((n_batch, MLA_QK_PAD, seq), BF16),
        jax.ShapeDtypeStruct((rows, MLA_QK_PAD), BF16),
        jax.ShapeDtypeStruct((n_batch, MLA_WIDTH, seq), BF16),
    ]
    out_specs = [row_spec(CONV_W), row_spec(CONV_W), feat_spec(DIFF_QK_COLS), row_spec(DIFF_QK_COLS),
                 feat_spec(DIFF_WIDTH), feat_spec(MLA_QK_PAD), row_spec(MLA_QK_PAD), feat_spec(MLA_WIDTH)]
    aliases = {}
    if with_state:
        assert n_seq >= 1
        n_out = len(out_shape)
        out_shape += [
            jax.ShapeDtypeStruct((n_batch, n_layers, DIFF_HEADS, 2, seq, DIFF_DK), F32),
            jax.ShapeDtypeStruct((n_batch, n_layers, DIFF_HEADS, seq, DIFF_DV), F32),
            jax.ShapeDtypeStruct((n_batch, n_layers, seq, MLA_KV_RANK), F32),
            jax.ShapeDtypeStruct((n_batch, n_layers, seq, MLA_ROPE), F32),
        ]
        lyr, l_idx = (None, l) if alias_state else (n_layers, 0)
        out_specs += [
            pl.BlockSpec((n_seq, lyr, DIFF_HEADS, 2, seq, DIFF_DK), lambda i: (i, l_idx, 0, 0, 0, 0)),
            pl.BlockSpec((n_seq, lyr, DIFF_HEADS, seq, DIFF_DV), lambda i: (i, l_idx, 0, 0, 0)),
            pl.BlockSpec((n_seq, lyr, seq, MLA_KV_RANK), lambda i: (i, l_idx, 0, 0)),
            pl.BlockSpec((n_seq, lyr, seq, MLA_ROPE), lambda i: (i, l_idx, 0, 0)),
        ]
        if alias_state:
            for j, st in enumerate(states):
                aliases[len(args)] = n_out + j
                in_specs.append(pl.BlockSpec(memory_space=pl.ANY))
                args.append(st)

    kern = functools.partial(_inproj_kernel, rope=rope, n_seq=n_seq, seq=seq, with_state=with_state,
                             alias_state=alias_state)
    return pl.pallas_call(
        kern, out_shape=tuple(out_shape), grid=(n_tiles,),
        in_specs=in_specs, out_specs=tuple(out_specs), input_output_aliases=aliases,
        compiler_params=_cparams(1), name=f"inproj_l{l}_{'s' if rope else 'p'}",
    )(*args)


def _tree(op, xs):
    xs = list(xs)
    while len(xs) > 1:
        xs = [op(xs[i], xs[i + 1]) if i + 1 < len(xs) else xs[i] for i in range(0, len(xs), 2)]
    return xs[0]


def _slabs(x):
    return [x[j * SUBLANES:(j + 1) * SUBLANES] for j in range(x.shape[0] // SUBLANES)]


class _Head:
    def __init__(self, k_refs, rhs_fn, vt_refs, scale_log2e):
        self.k_blocks = [(k, r0) for k in k_refs for r0 in range(0, k.shape[0], MXU_DIM)]
        self.v_blocks = [(v, c0) for v in vt_refs for c0 in range(0, v.shape[1], MXU_DIM)]
        self.rhs_fn, self.scale = rhs_fn, scale_log2e


def _run_heads(heads, finish, zero_ref, s_ring, p_ring):
    items = [(h, j) for h in range(len(heads)) for j in range(len(heads[h].k_blocks))]
    z_st = pl.multiple_of(zero_ref[0], MXU_DIM)
    z_ld = pl.multiple_of(zero_ref[1], MXU_DIM)
    rhs = {}
    blk_max = {}
    state = {}

    def issue_scores(i):
        h, j = items[i]
        if j == 0:
            rhs[h] = heads[h].rhs_fn()
        k_ref, r0 = heads[h].k_blocks[j]
        s = jnp.dot(k_ref[r0:r0 + MXU_DIM, :], rhs[h], preferred_element_type=F32)
        s_ring[i % len(s_ring)][pl.ds(z_st, MXU_DIM), :] = s
        blk_max[i] = _tree(jnp.maximum, _slabs(s))

    def consume(i):
        h, j = items[i]
        head = heads[h]
        s_buf, p_buf = s_ring[i % len(s_ring)], p_ring[i % len(p_ring)]
        m_blk = jnp.max(blk_max.pop(i), axis=0, keepdims=True)
        m_new = m_blk if j == 0 else jnp.maximum(state["m"], m_blk)
        for c0 in range(0, MXU_DIM, KEY_CHUNK):
            p = jnp.exp2((s_buf[pl.ds(z_ld + c0, KEY_CHUNK), :] - m_new) * head.scale)
            p_buf[pl.ds(z_st + c0, KEY_CHUNK), :] = p.astype(BF16)
        v_ref, c0 = head.v_blocks[j]
        v_ext = jnp.concatenate([v_ref[:, c0:c0 + MXU_DIM], jnp.ones((BF16_ROWS, MXU_DIM), BF16)], axis=0)
        pv = jnp.dot(v_ext, p_buf[pl.ds(z_ld, MXU_DIM), :], preferred_element_type=F32)
        if j == 0:
            state["acc"] = pv
        else:
            alpha = jnp.exp2((state["m"] - m_new) * head.scale)
            state["acc"] = state["acc"] * alpha + pv
        state["m"] = m_new
        if j == len(head.k_blocks) - 1:
            acc = state["acc"]
            dv = acc.shape[0] - BF16_ROWS
            finish(h, acc[:dv] * (1.0 / acc[dv:dv + 1]))

    for i in range(len(items) + LOOKAHEAD):
        if i < len(items):
            issue_scores(i)
        if i >= LOOKAHEAD:
            consume(i - LOOKAHEAD)


def _attn_kernel(*refs, lam_init, has_cache):
    it = iter(refs)
    zero_ref = next(it)
    lam_ref, dnw_ref = next(it), next(it)
    qdt_ref, kd_ref, vdt_ref, qmt_ref, kc_ref, vmt_ref = (next(it) for _ in range(6))
    kds, vdts, kcs, vmts = [kd_ref], [vdt_ref], [kc_ref], [vmt_ref]
    if has_cache:
        kds.append(next(it)); vdts.append(next(it)); kcs.append(next(it)); vmts.append(next(it))
    yb_ref, yc_ref = next(it), next(it)
    s_ring = [next(it) for _ in range(LOOKAHEAD + 1)]
    p_ring = [next(it) for _ in range(2)]

    lv = lam_ref[0]
    lam = (jnp.exp(jnp.sum(lv[0:1] * lv[1:2], axis=-1, keepdims=True))
           - jnp.exp(jnp.sum(lv[2:3] * lv[3:4], axis=-1, keepdims=True)) + lam_init)

    def diff_rhs(hm):
        def fn():
            q_dt = qdt_ref[...]
            feat = lax.broadcasted_iota(jnp.int32, q_dt.shape, 0)
            return jnp.where((feat >= hm * DIFF_DK) & (feat < (hm + 1) * DIFF_DK), q_dt, jnp.zeros_like(q_dt))
        return fn

    def mla_rhs(feats):
        return lambda: qmt_ref[feats, :]

    heads = []
    for hm in range(2 * DIFF_HEADS):
        rows = slice((hm // 2) * DIFF_DV, (hm // 2 + 1) * DIFF_DV)
        heads.append(_Head(kds, diff_rhs(hm), [vt.at[rows, :] for vt in vdts], DIFF_SCALE * LOG2E))
    for hd in range(MLA_HEADS):
        feats = slice(hd * MLA_HEAD_PAD, (hd + 1) * MLA_HEAD_PAD)
        rows = slice(hd * MLA_V, (hd + 1) * MLA_V)
        heads.append(_Head([kc.at[:, feats] for kc in kcs], mla_rhs(feats), [vt.at[rows, :] for vt in vmts],
                           MLA_SCALE * LOG2E))

    outs = []

    def finish(t, o):
        outs.append(o)
        if t == 2 * DIFF_HEADS - 1:
            y_heads = []
            for hd in range(DIFF_HEADS):
                d = outs[2 * hd] - lam * outs[2 * hd + 1]
                ms = jnp.mean(d * d, axis=0, keepdims=True)
                y_heads.append(d * lax.rsqrt(ms + 1e-6))
            y_b = jnp.concatenate(y_heads, axis=0).T
            yb_ref[...] = (y_b * dnw_ref[0] * (1.0 - lam_init)).astype(BF16)
        if t == len(heads) - 1:
            yc_ref[...] = jnp.concatenate(outs[2 * DIFF_HEADS:], axis=0).T.astype(BF16)

    _run_heads(heads, finish, zero_ref, s_ring, p_ring)


def _attn_call(lamv, dnw, qdt, kd, vdt, qmt, kc, vmt, cache, l, *, n_batch, seq, tq, lam_init):
    t_per_seq = seq // tq
    has_cache = cache is not None

    def q_spec(width):
        return pl.BlockSpec((None, width, tq), lambda b, t, z: (b, 0, t))

    def out_spec(width):
        return pl.BlockSpec((tq, width), lambda b, t, z: (b * t_per_seq + t, 0))

    in_specs = [
        pl.BlockSpec((1, 4, DIFF_DK), lambda b, t, z: (l, 0, 0)),
        pl.BlockSpec((1, 1, DIFF_WIDTH), lambda b, t, z: (l, 0, 0)),
        q_spec(DIFF_QK_COLS),
        pl.BlockSpec((seq, DIFF_QK_COLS), lambda b, t, z: (b, 0)),
        pl.BlockSpec((None, DIFF_WIDTH, seq), lambda b, t, z: (b, 0, 0)),
        q_spec(MLA_QK_PAD),
        pl.BlockSpec((seq, MLA_QK_PAD), lambda b, t, z: (b, 0)),
        pl.BlockSpec((None, MLA_WIDTH, seq), lambda b, t, z: (b, 0, 0)),
    ]
    args = [lamv, dnw, qdt, kd, vdt, qmt, kc, vmt]
    past = 0
    if has_cache:
        past = cache[0].shape[2]
        in_specs += [
            pl.BlockSpec((None, None, past, DIFF_QK_COLS), lambda b, t, z: (b, l, 0, 0)),
            pl.BlockSpec((None, None, DIFF_WIDTH, past), lambda b, t, z: (b, l, 0, 0)),
            pl.BlockSpec((None, None, past, MLA_QK_PAD), lambda b, t, z: (b, l, 0, 0)),
            pl.BlockSpec((None, None, MLA_WIDTH, past), lambda b, t, z: (b, l, 0, 0)),
        ]
        args += list(cache)
    rows = n_batch * seq
    ring = (2 * MXU_DIM, tq)
    return pl.pallas_call(
        functools.partial(_attn_kernel, lam_init=lam_init, has_cache=has_cache),
        out_shape=(jax.ShapeDtypeStruct((rows, DIFF_WIDTH), BF16),
                   jax.ShapeDtypeStruct((rows, MLA_WIDTH), BF16)),
        grid_spec=pltpu.PrefetchScalarGridSpec(
            num_scalar_prefetch=1,
            grid=(n_batch, t_per_seq),
            in_specs=in_specs,
            out_specs=(out_spec(DIFF_WIDTH), out_spec(MLA_WIDTH)),
            scratch_shapes=[pltpu.VMEM(ring, F32)] * (LOOKAHEAD + 1) + [pltpu.VMEM(ring, BF16)] * 2),
        compiler_params=_cparams(2), name=f"attn_l{l}_{'s' if has_cache else 'p'}",
    )(jnp.zeros((2,), jnp.int32), *args)


def _tail_kernel(x_ref, u_ref, up_ref, un_ref, ab_ref, yb_ref, yc_ref, cw_ref, g1_ref, sh2_ref, sc2_ref, g2_ref,
                 ln1g_ref, ln1b_ref, ln2g_ref, ln2b_ref, w_out_ref, w1_ref, w3_ref, w2_ref, o_ref, *, seq):
    tm = x_ref.shape[0]
    u = u_ref[...]
    row = lax.broadcasted_iota(jnp.int32, (tm, 1), 0)
    pos = (row + pl.program_id(0) * tm) % seq
    u_prev = jnp.where(row == 0, up_ref[SUBLANES - 1:SUBLANES, :], pltpu.roll(u, 1, axis=0))
    u_prev = jnp.where(pos == 0, 0.0, u_prev)
    u_next = jnp.where(row == tm - 1, un_ref[0:1, :], pltpu.roll(u, tm - 1, axis=0))
    u_next = jnp.where(pos == seq - 1, 0.0, u_next)
    cw = cw_ref[0]
    y_a = ab_ref[...] * (u_prev * cw[0:1] + u * cw[1:2] + u_next * cw[2:3])

    y_cat = jnp.concatenate([y_a.astype(BF16), yb_ref[...], yc_ref[...]], axis=1)
    y = jnp.dot(y_cat, w_out_ref[...], preferred_element_type=F32)
    x1 = _layer_norm(DEEPNORM_ALPHA * x_ref[...] + g1_ref[0] * y, ln1g_ref[0], ln1b_ref[0])

    hf = (x1 * (1.0 + sc2_ref[0]) + sh2_ref[0]).astype(BF16)
    f = None
    for c0, c1 in FF_CHUNKS:
        g = jnp.dot(hf, w1_ref[:, c0:c1], preferred_element_type=F32)
        a = (g * _sigmoid(g) * jnp.dot(hf, w3_ref[:, c0:c1], preferred_element_type=F32)).astype(BF16)
        t = jnp.dot(a, w2_ref[c0:c1, :], preferred_element_type=F32)
        f = t if f is None else f + t
    o_ref[...] = _layer_norm(DEEPNORM_ALPHA * x1 + g2_ref[0] * f, ln2g_ref[0], ln2b_ref[0])


def _tail_call(x2d, u, ab, yb, yc, mod3, wts, l, *, seq, tm, mod_row):
    rows = x2d.shape[0]
    n_tiles = rows // tm
    n_halo = rows // SUBLANES

    def mod_spec(j):
        return pl.BlockSpec((1, 1, D_MODEL), lambda i: ((l * COND_ROWS + mod_row(i)) * N_MOD + j, 0, 0))

    def row_spec(width):
        return pl.BlockSpec((tm, width), lambda i: (i, 0))

    in_specs = [
        row_spec(D_MODEL), row_spec(CONV_W),
        pl.BlockSpec((SUBLANES, CONV_W), lambda i: (jnp.maximum(i * (tm // SUBLANES) - 1, 0), 0)),
        pl.BlockSpec((SUBLANES, CONV_W), lambda i: (jnp.minimum((i + 1) * (tm // SUBLANES), n_halo - 1), 0)),
        row_spec(CONV_W), row_spec(DIFF_WIDTH), row_spec(MLA_WIDTH),
        pl.BlockSpec((1, 3, CONV_W), lambda i: (l, 0, 0)),
        mod_spec(2), mod_spec(3), mod_spec(4), mod_spec(5),
        _layer_spec((1, D_MODEL), l), _layer_spec((1, D_MODEL), l),
        _layer_spec((1, D_MODEL), l), _layer_spec((1, D_MODEL), l),
        _layer_spec((D_MODEL, D_MODEL), l),
        _layer_spec((D_MODEL, D_FF), l), _layer_spec((D_MODEL, D_FF), l), _layer_spec((D_FF, D_MODEL), l),
    ]
    return pl.pallas_call(
        functools.partial(_tail_kernel, seq=seq),
        out_shape=jax.ShapeDtypeStruct((rows, D_MODEL), F32),
        grid=(n_tiles,),
        in_specs=in_specs,
        out_specs=row_spec(D_MODEL),
        compiler_params=_cparams(1), name=f"tail_l{l}",
    )(x2d, u, u, u, ab, yb, yc, wts["conv_w"], mod3, mod3, mod3, mod3,
      wts["ln1_g"], wts["ln1_b"], wts["ln2_g"], wts["ln2_b"],
      wts["w_out"], wts["w_ff1"], wts["w_ff3"], wts["w_ff2"])


def _rope_tables(n_tokens):
    rows = n_tokens // GRID_W
    row = np.repeat(np.arange(rows), GRID_W).astype(np.float32)
    col = np.tile(np.arange(GRID_W), rows).astype(np.float32)
    half = MLA_ROPE // 4
    freqs = (np.float32(ROPE_BASE) ** (-np.arange(half, dtype=np.float32) / np.float32(half))).astype(np.float32)
    ang_r = row[:, None] * freqs[None, :]
    ang_c = col[:, None] * freqs[None, :]
    cos32 = np.concatenate([np.cos(ang_r), np.cos(ang_r), np.cos(ang_c), np.cos(ang_c)], axis=1)
    sin32 = np.concatenate([-np.sin(ang_r), np.sin(ang_r), -np.sin(ang_c), np.sin(ang_c)], axis=1)
    one = np.ones((n_tokens, 1), np.float32)
    zero = np.zeros((n_tokens, 1), np.float32)
    cos_d, sin_d = np.tile(cos32, (1, 4)), np.tile(sin32, (1, 4))
    cos_m = np.concatenate([np.tile(one, (1, MLA_NOPE)), cos32, np.tile(one, (1, 32))], axis=1)
    sin_m = np.concatenate([np.tile(zero, (1, MLA_NOPE)), sin32, np.tile(zero, (1, 32))], axis=1)
    cos_p = np.concatenate([cos32, np.tile(one, (1, 96))], axis=1)
    sin_p = np.concatenate([sin32, np.tile(zero, (1, 96))], axis=1)
    return tuple(jnp.asarray(t.astype(np.float32)) for t in (cos_d, sin_d, cos_m, sin_m, cos_p, sin_p))


def _prep_weights(w_in, w_uq, w_ukv, w_out, w_ff1, w_ff3, w_ff2, conv_w, q_norm_w, kv_norm_w,
                  ln1_g, ln1_b, ln2_g, ln2_b):
    n_l = w_in.shape[0]
    w_in_p = jnp.pad(w_in, ((0, 0), (0, 0), (0, IN_COLS_PAD - IN_COLS))).astype(BF16)
    w_uq_p = jnp.pad(w_uq.reshape(n_l, MLA_Q_RANK, MLA_HEADS, MLA_QK),
                     ((0, 0), (0, 0), (0, 0), (0, MLA_HEAD_PAD - MLA_QK))).reshape(n_l, MLA_Q_RANK, MLA_QK_PAD)
    w_ukv4 = w_ukv.reshape(n_l, MLA_KV_RANK, MLA_HEADS, MLA_NOPE + MLA_V)
    w_uk_p = jnp.pad(w_ukv4[..., :MLA_NOPE], ((0, 0), (0, 0), (0, 0), (0, MLA_HEAD_PAD - MLA_NOPE)))
    w_kv = jnp.concatenate([w_uk_p.reshape(n_l, MLA_KV_RANK, MLA_QK_PAD),
                            w_ukv4[..., MLA_NOPE:].reshape(n_l, MLA_KV_RANK, MLA_WIDTH)], axis=-1)
    vec = lambda a: a.reshape(n_l, 1, a.shape[-1])
    return {
        "w_in": w_in_p, "w_uq": w_uq_p.astype(BF16), "w_kv": w_kv.astype(BF16),
        "w_out": w_out.astype(BF16), "w_ff1": w_ff1.astype(BF16), "w_ff3": w_ff3.astype(BF16),
        "w_ff2": w_ff2.astype(BF16), "conv_w": conv_w,
        "q_norm_w": vec(q_norm_w), "kv_norm_w": vec(kv_norm_w),
        "ln1_g": vec(ln1_g), "ln1_b": vec(ln1_b), "ln2_g": vec(ln2_g), "ln2_b": vec(ln2_b),
    }


def kernel(x_prompt, x_sample, cache_diff_k, cache_diff_v, cache_mla_ckv, cache_mla_kpe, c, c_ctx,
           w_ada, b_ada, w_in, conv_w, lam_q1, lam_k1, lam_q2, lam_k2, diff_norm_w, q_norm_w, w_uq,
           kv_norm_w, w_ukv, w_out, ln1_g, ln1_b, w_ff1, w_ff3, w_ff2, ln2_g, ln2_b):
    n_l = w_in.shape[0]
    pb, ps, _ = x_prompt.shape
    sb, ss, _ = x_sample.shape
    past = cache_mla_ckv.shape[2]

    wts = _prep_weights(w_in, w_uq, w_ukv, w_out, w_ff1, w_ff3, w_ff2, conv_w, q_norm_w, kv_norm_w,
                        ln1_g, ln1_b, ln2_g, ln2_b)
    lamv = jnp.stack([lam_q1, lam_k1, lam_q2, lam_k2], axis=1)
    dnw = jnp.tile(diff_norm_w, (1, DIFF_HEADS)).reshape(n_l, 1, DIFF_WIDTH)

    cond = jnp.concatenate([c_ctx[None, :], c, jnp.zeros((COND_ROWS - 1 - sb, D_MODEL), F32)], axis=0)
    mod3 = _ada_call(cond, w_ada, b_ada).reshape(n_l * COND_ROWS * N_MOD, 1, D_MODEL)

    ck = jnp.transpose(cache_diff_k, (0, 1, 4, 2, 3, 5)).reshape(sb, n_l, past, DIFF_QK_COLS).astype(BF16)
    cv_t = jnp.swapaxes(cache_diff_v, -1, -2).reshape(sb, n_l, DIFF_WIDTH, past).astype(BF16)
    kpe_slab = jnp.pad(cache_mla_kpe, ((0, 0), (0, 0), (0, 0), (0, LANES - MLA_ROPE)))
    ckc, cvm_t = _cache_kv_call(cache_mla_ckv, kpe_slab, wts["w_kv"])

    rope_tabs = _rope_tables(ss)
    tm_p, tm_s, tq_s = 512, 512, 512

    xp = x_prompt.reshape(pb * ps, D_MODEL)
    xs = x_sample.reshape(sb * ss, D_MODEL)
    states = ()
    for l in range(n_l):
        lam_init = 0.8 - 0.6 * math.exp(-0.3 * l)

        row_p = lambda i: 0
        outs = _inproj_call(xp, mod3, wts, l, n_batch=pb, seq=ps, tm=tm_p, mod_row=row_p, rope_tabs=None,
                            n_layers=n_l, states=states)
        u, ab, qdt, kd, vdt, qmt, kc, vmt = outs[:8]
        states = tuple(outs[8:])
        yb, yc = _attn_call(lamv, dnw, qdt, kd, vdt, qmt, kc, vmt, None, l,
                            n_batch=pb, seq=ps, tq=ps, lam_init=lam_init)
        xp = _tail_call(xp, u, ab, yb, yc, mod3, wts, l, seq=ps, tm=tm_p, mod_row=row_p)

        row_s = lambda i: 1 + i // (ss // tm_s)
        u, ab, qdt, kd, vdt, qmt, kc, vmt = _inproj_call(
            xs, mod3, wts, l, n_batch=sb, seq=ss, tm=tm_s, mod_row=row_s, rope_tabs=rope_tabs,
            n_layers=n_l, states=None)
        yb, yc = _attn_call(lamv, dnw, qdt, kd, vdt, qmt, kc, vmt, (ck, cv_t, ckc, cvm_t), l,
                            n_batch=sb, seq=ss, tq=tq_s, lam_init=lam_init)
        xs = _tail_call(xs, u, ab, yb, yc, mod3, wts, l, seq=ss, tm=tm_s, mod_row=row_s)

    return (xp.reshape(pb, ps, D_MODEL), xs.reshape(sb, ss, D_MODEL)) + states
```

```python
import functools
import math

import numpy as np
import jax
import jax.numpy as jnp
from jax import lax
from jax.experimental import pallas as pl
from jax.experimental.pallas import tpu as pltpu

F32 = jnp.float32
BF16 = jnp.bfloat16

D_MODEL = 1024
DEPTH = 2
GRID_W = 64
ROPE_BASE = 10000.0
CONV_W = 256
DIFF_HEADS = 4
DIFF_DK = 32
DIFF_DV = 64
DIFF_WIDTH = DIFF_HEADS * DIFF_DV
DIFF_QK_COLS = DIFF_HEADS * 2 * DIFF_DK
DIFF_SCALE = DIFF_DK ** -0.5
MLA_HEADS = 8
MLA_Q_RANK = 384
MLA_KV_RANK = 256
MLA_NOPE = 64
MLA_ROPE = 32
MLA_V = 64
MLA_QK = MLA_NOPE + MLA_ROPE
MLA_WIDTH = MLA_HEADS * MLA_V
MLA_SCALE = MLA_QK ** -0.5
IN_COLS = 3 * CONV_W + 2 * DIFF_QK_COLS + DIFF_WIDTH + MLA_Q_RANK + MLA_KV_RANK + MLA_ROPE
D_FF = 2816
DEEPNORM_ALPHA = (2 * DEPTH) ** 0.25
LOG2E = 1.4426950408889634

LANES = 128
SUBLANES = 8
BF16_ROWS = 16
MXU_DIM = 256
IN_COLS_PAD = 2304
MLA_HEAD_PAD = LANES
MLA_QK_PAD = MLA_HEADS * MLA_HEAD_PAD
N_MOD = 6
COND_ROWS = 8
FF_CHUNKS = ((0, 1536), (1536, 2816))
KEY_CHUNK = 64
LOOKAHEAD = 3
TAIL_SUB = 256
INPROJ_SUB = 256
VMEM_LIMIT = 56 * 2 ** 20

OFF_AX, OFF_AB, OFF_AC = 0, 256, 512
OFF_DQ, OFF_DK, OFF_DV = 768, 1024, 1280
OFF_CQ, OFF_CKV, OFF_KPE = 1536, 1920, 2176


def _cparams(n_grid):
    return pltpu.CompilerParams(dimension_semantics=("arbitrary",) * n_grid,
                                vmem_limit_bytes=VMEM_LIMIT)


def _layer_spec(tail, l):
    nd = len(tail)
    return pl.BlockSpec((None,) + tuple(tail), lambda *_: (l,) + (0,) * nd,
                        pipeline_mode=pl.Buffered(1))


def _sigmoid(x):
    return 1.0 / (1.0 + jnp.exp(-x))


def _layer_norm(x, g, b):
    mu = jnp.mean(x, axis=-1, keepdims=True)
    xc = x - mu
    var = jnp.mean(xc * xc, axis=-1, keepdims=True)
    return xc * lax.rsqrt(var + 1e-5) * g + b


def _rms_norm(x, w):
    ms = jnp.mean(x * x, axis=-1, keepdims=True)
    return x * lax.rsqrt(ms + 1e-6) * w


def _rope_slab(x, cos, sin_signed):
    lane = lax.broadcasted_iota(jnp.int32, x.shape, 1)
    lo = (lane % 16) < 8
    partner = jnp.where(lo, pltpu.roll(x, LANES - 8, axis=1), pltpu.roll(x, 8, axis=1))
    return x * cos + partner * sin_signed


def _ada_kernel(cond_ref, w_ref, b_ref, o_ref):
    c = cond_ref[...]
    s = (c * _sigmoid(c)).astype(BF16)
    o_ref[0] = jnp.dot(s, w_ref[0].astype(BF16), preferred_element_type=F32) + b_ref[0]


def _ada_call(cond, w_ada, b_ada):
    n_l = w_ada.shape[0]
    tn = D_MODEL
    return pl.pallas_call(
        _ada_kernel,
        out_shape=jax.ShapeDtypeStruct((n_l, COND_ROWS, N_MOD * D_MODEL), F32),
        grid=(n_l, N_MOD * D_MODEL // tn),
        in_specs=[
            pl.BlockSpec((COND_ROWS, D_MODEL), lambda l, j: (0, 0)),
            pl.BlockSpec((1, D_MODEL, tn), lambda l, j: (l, 0, j)),
            pl.BlockSpec((1, 1, tn), lambda l, j: (l, 0, j)),
        ],
        out_specs=pl.BlockSpec((1, COND_ROWS, tn), lambda l, j: (l, 0, j)),
        compiler_params=_cparams(2),
        name="ada_mod",
    )(cond, w_ada, b_ada.reshape(n_l, 1, N_MOD * D_MODEL))


def _expand_kv(ckv, kpe_slab, w_kv_ref):
    kv = jnp.dot(ckv.astype(BF16), w_kv_ref[...], preferred_element_type=F32)
    kpe_shift = pltpu.roll(kpe_slab, MLA_NOPE, axis=1)
    k_cat = jnp.concatenate(
        [(kv[:, h * MLA_HEAD_PAD:(h + 1) * MLA_HEAD_PAD] + kpe_shift).astype(BF16) for h in range(MLA_HEADS)],
        axis=1)
    return k_cat, kv[:, MLA_QK_PAD:MLA_QK_PAD + MLA_WIDTH]


def _cache_kv_kernel(ckv_ref, kpe_ref, w_kv_ref, kc_ref, vmt_ref):
    k_cat, v_m = _expand_kv(ckv_ref[...], kpe_ref[...], w_kv_ref)
    kc_ref[...] = k_cat
    vmt_ref[...] = v_m.T.astype(BF16)


def _cache_kv_call(cache_ckv, cache_kpe_slab, w_kv):
    n_b, n_l, past, _ = cache_ckv.shape
    return pl.pallas_call(
        _cache_kv_kernel,
        out_shape=(jax.ShapeDtypeStruct((n_b, n_l, past, MLA_QK_PAD), BF16),
                   jax.ShapeDtypeStruct((n_b, n_l, MLA_WIDTH, past), BF16)),
        grid=(n_b, n_l),
        in_specs=[
            pl.BlockSpec((None, None, past, MLA_KV_RANK), lambda b, l: (b, l, 0, 0)),
            pl.BlockSpec((None, None, past, LANES), lambda b, l: (b, l, 0, 0)),
            pl.BlockSpec((None, MLA_KV_RANK, MLA_QK_PAD + MLA_WIDTH), lambda b, l: (l, 0, 0)),
        ],
        out_specs=(pl.BlockSpec((None, None, past, MLA_QK_PAD), lambda b, l: (b, l, 0, 0)),
                   pl.BlockSpec((None, None, MLA_WIDTH, past), lambda b, l: (b, l, 0, 0))),
        compiler_params=_cparams(2),
        name="cache_kv",
    )(cache_ckv, cache_kpe_slab, w_kv)


N_STATE = 4


def _inproj_kernel(*refs, rope, n_seq, seq, with_state, alias_state):
    it = iter(refs)
    x_ref, sc_ref, sh_ref = next(it), next(it), next(it)
    w_in_ref, qnw_ref, w_uq_ref, kvnw_ref, w_kv_ref = next(it), next(it), next(it), next(it), next(it)
    if rope:
        cd_ref, sd_ref, cm_ref, sm_ref, cp_ref, sp_ref = (next(it) for _ in range(6))
    if alias_state:
        for _ in range(N_STATE):
            next(it)
    u_ref, ab_ref, qdt_ref, kd_ref, vdt_ref, qmt_ref, kc_ref, vmt_ref = (next(it) for _ in range(8))
    if with_state:
        stk_ref, stv_ref, stc_ref, stp_ref = (next(it) for _ in range(N_STATE))

    tm = x_ref.shape[0]
    sub = seq if n_seq else INPROJ_SUB
    st = [dict(r=r, rows=slice(r * sub, (r + 1) * sub)) for r in range(tm // sub)]

    def project(d):
        h = (x_ref[d["rows"], :] * (1.0 + sc_ref[0]) + sh_ref[0]).astype(BF16)
        d["proj"] = jnp.dot(h, w_in_ref[...], preferred_element_type=F32)

    def split(d):
        proj, rows = d.pop("proj"), d["rows"]
        u_ref[rows, :] = proj[:, OFF_AC:OFF_AC + CONV_W] * proj[:, OFF_AX:OFF_AX + CONV_W]
        ab_ref[rows, :] = proj[:, OFF_AB:OFF_AB + CONV_W]
        q = proj[:, OFF_DQ:OFF_DQ + DIFF_QK_COLS]
        k = proj[:, OFF_DK:OFF_DK + DIFF_QK_COLS]
        d["v"] = proj[:, OFF_DV:OFF_DV + DIFF_WIDTH]
        if rope:
            cd, sd = cd_ref[rows, :], sd_ref[rows, :]
            q = jnp.concatenate([_rope_slab(q[:, j * LANES:(j + 1) * LANES], cd, sd) for j in range(2)], axis=1)
            k = jnp.concatenate([_rope_slab(k[:, j * LANES:(j + 1) * LANES], cd, sd) for j in range(2)], axis=1)
        kd_ref[rows, :] = k.astype(BF16)
        d["q"], d["k"] = q, k
        d["cq"] = _rms_norm(proj[:, OFF_CQ:OFF_CQ + MLA_Q_RANK], qnw_ref[0]).astype(BF16)
        d["ckv"] = _rms_norm(proj[:, OFF_CKV:OFF_CKV + MLA_KV_RANK], kvnw_ref[0])
        kpe_slab = proj[:, OFF_KPE:OFF_KPE + LANES]
        if rope:
            kpe_slab = _rope_slab(kpe_slab, cp_ref[rows, :], sp_ref[rows, :])
        d["kpe"] = kpe_slab

    def up_project(d):
        d["qc"] = jnp.dot(d.pop("cq"), w_uq_ref[...], preferred_element_type=F32)
        d["k_cat"], d["v_m"] = _expand_kv(d["ckv"], d["kpe"], w_kv_ref)

    def emit(d):
        r, rows = d["r"], d["rows"]
        qc = d.pop("qc")
        if rope:
            cm, sm = cm_ref[rows, :], sm_ref[rows, :]
            qc = jnp.concatenate(
                [_rope_slab(qc[:, hd * MLA_HEAD_PAD:(hd + 1) * MLA_HEAD_PAD], cm, sm) for hd in range(MLA_HEADS)],
                axis=1)
        kc_ref[rows, :] = d.pop("k_cat")
        for dst, val in ((qdt_ref, d["q"]), (vdt_ref, d["v"]), (qmt_ref, qc), (vmt_ref, d.pop("v_m"))):
            val_t = val.T.astype(BF16)
            if n_seq:
                dst[r] = val_t
            else:
                dst[:, rows] = val_t
        if with_state:
            fills = [()] if alias_state else [(ll,) for ll in range(stc_ref.shape[1])]
            for ll in fills:
                stc_ref[(r,) + ll] = d["ckv"]
                stp_ref[(r,) + ll] = d["kpe"][:, :MLA_ROPE]
                for hm in range(2 * DIFF_HEADS):
                    stk_ref[(r,) + ll + (hm // 2, hm % 2)] = d["k"][:, hm * DIFF_DK:(hm + 1) * DIFF_DK]
                for hd in range(DIFF_HEADS):
                    stv_ref[(r,) + ll + (hd,)] = d["v"][:, hd * DIFF_DV:(hd + 1) * DIFF_DV]

    stages = [project, split, up_project, emit]
    for step in range(len(stages) + len(st) - 1):
        for r, d in enumerate(st):
            if 0 <= step - r < len(stages):
                stages[step - r](d)


def _inproj_call(x2d, mod3, wts, l, *, n_batch, seq, tm, mod_row, rope_tabs, n_layers, states):
    rows = n_batch * seq
    n_tiles = rows // tm
    rope = rope_tabs is not None
    with_state = states is not None
    alias_state = bool(states)
    n_seq = tm // seq
    t_per_seq = max(seq // tm, 1)

    def mod_spec(j):
        return pl.BlockSpec((1, 1, D_MODEL), lambda i: ((l * COND_ROWS + mod_row(i)) * N_MOD + j, 0, 0))

    def row_spec(width):
        return pl.BlockSpec((tm, width), lambda i: (i, 0))

    def feat_spec(width):
        if n_seq:
            return pl.BlockSpec((n_seq, width, seq), lambda i: (i, 0, 0))
        return pl.BlockSpec((None, width, tm), lambda i: (i // t_per_seq, 0, i % t_per_seq))

    in_specs = [
        row_spec(D_MODEL), mod_spec(1), mod_spec(0),
        _layer_spec((D_MODEL, IN_COLS_PAD), l),
        _layer_spec((1, MLA_Q_RANK), l),
        _layer_spec((MLA_Q_RANK, MLA_QK_PAD), l),
        _layer_spec((1, MLA_KV_RANK), l),
        _layer_spec((MLA_KV_RANK, MLA_QK_PAD + MLA_WIDTH), l),
    ]
    args = [x2d, mod3, mod3, wts["w_in"], wts["q_norm_w"], wts["w_uq"], wts["kv_norm_w"], wts["w_kv"]]
    if rope:
        in_specs += [pl.BlockSpec((tm, LANES), lambda i: (i % t_per_seq, 0))] * 6
        args += list(rope_tabs)

    out_shape = [
        jax.ShapeDtypeStruct((rows, CONV_W), F32),
        jax.ShapeDtypeStruct((rows, CONV_W), F32),
        jax.ShapeDtypeStruct((n_batch, DIFF_QK_COLS, seq), BF16),
        jax.ShapeDtypeStruct((rows, DIFF_QK_COLS), BF16),
        jax.ShapeDtypeStruct((n_batch, DIFF_WIDTH, seq), BF16),
        jax.ShapeDtypeStruct((n_batch, MLA_QK_PAD, seq), BF16),
        jax.ShapeDtypeStruct((rows, MLA_QK_PAD), BF16),
        jax.ShapeDtypeStruct((n_batch, MLA_WIDTH, seq), BF16),
    ]
    out_specs = [row_spec(CONV_W), row_spec(CONV_W), feat_spec(DIFF_QK_COLS), row_spec(DIFF_QK_COLS),
                 feat_spec(DIFF_WIDTH), feat_spec(MLA_QK_PAD), row_spec(MLA_QK_PAD), feat_spec(MLA_WIDTH)]
    aliases = {}
    if with_state:
        assert n_seq >= 1
        n_out = len(out_shape)
        out_shape += [
            jax.ShapeDtypeStruct((n_batch, n_layers, DIFF_HEADS, 2, seq, DIFF_DK), F32),
            jax.ShapeDtypeStruct((n_batch, n_layers, DIFF_HEADS, seq, DIFF_DV), F32),
            jax.ShapeDtypeStruct((n_batch, n_layers, seq, MLA_KV_RANK), F32),
            jax.ShapeDtypeStruct((n_batch, n_layers, seq, MLA_ROPE), F32),
        ]
        lyr, l_idx = (None, l) if alias_state else (n_layers, 0)
        out_specs += [
            pl.BlockSpec((n_seq, lyr, DIFF_HEADS, 2, seq, DIFF_DK), lambda i: (i, l_idx, 0, 0, 0, 0)),
            pl.BlockSpec((n_seq, lyr, DIFF_HEADS, seq, DIFF_DV), lambda i: (i, l_idx, 0, 0, 0)),
            pl.BlockSpec((n_seq, lyr, seq, MLA_KV_RANK), lambda i: (i, l_idx, 0, 0)),
            pl.BlockSpec((n_seq, lyr, seq, MLA_ROPE), lambda i: (i, l_idx, 0, 0)),
        ]
        if alias_state:
            for j, st in enumerate(states):
                aliases[len(args)] = n_out + j
                in_specs.append(pl.BlockSpec(memory_space=pl.ANY))
                args.append(st)

    kern = functools.partial(_inproj_kernel, rope=rope, n_seq=n_seq, seq=seq, with_state=with_state,
                             alias_state=alias_state)
    return pl.pallas_call(
        kern, out_shape=tuple(out_shape), grid=(n_tiles,),
        in_specs=in_specs, out_specs=tuple(out_specs), input_output_aliases=aliases,
        compiler_params=_cparams(1), name=f"inproj_l{l}_{'s' if rope else 'p'}",
    )(*args)


def _tree(op, xs):
    xs = list(xs)
    while len(xs) > 1:
        xs = [op(xs[i], xs[i + 1]) if i + 1 < len(xs) else xs[i] for i in range(0, len(xs), 2)]
    return xs[0]


def _slabs(x):
    return [x[j * SUBLANES:(j + 1) * SUBLANES] for j in range(x.shape[0] // SUBLANES)]


class _Head:
    def __init__(self, k_refs, rhs_fn, vt_refs, scale_log2e):
        self.k_blocks = [(k, r0) for k in k_refs for r0 in range(0, k.shape[0], MXU_DIM)]
        self.v_blocks = [(v, c0) for v in vt_refs for c0 in range(0, v.shape[1], MXU_DIM)]
        self.rhs_fn, self.scale = rhs_fn, scale_log2e


def _run_heads(heads, finish, zero_ref, s_ring, p_ring):
    items = [(h, j) for h in range(len(heads)) for j in range(len(heads[h].k_blocks))]
    z_st = pl.multiple_of(zero_ref[0], MXU_DIM)
    z_ld = pl.multiple_of(zero_ref[1], MXU_DIM)
    rhs = {}
    blk_max = {}
    state = {}

    def issue_scores(i):
        h, j = items[i]
        if j == 0:
            rhs[h] = heads[h].rhs_fn()
        k_ref, r0 = heads[h].k_blocks[j]
        s = jnp.dot(k_ref[r0:r0 + MXU_DIM, :], rhs[h], preferred_element_type=F32)
        s_ring[i % len(s_ring)][pl.ds(z_st, MXU_DIM), :] = s
        blk_max[i] = _tree(jnp.maximum, _slabs(s))

    def consume(i):
        h, j = items[i]
        head = heads[h]
        s_buf, p_buf = s_ring[i % len(s_ring)], p_ring[i % len(p_ring)]
        m_blk = jnp.max(blk_max.pop(i), axis=0, keepdims=True)
        m_new = m_blk if j == 0 else jnp.maximum(state["m"], m_blk)
        for c0 in range(0, MXU_DIM, KEY_CHUNK):
            p = jnp.exp2((s_buf[pl.ds(z_ld + c0, KEY_CHUNK), :] - m_new) * head.scale)
            p_buf[pl.ds(z_st + c0, KEY_CHUNK), :] = p.astype(BF16)
        v_ref, c0 = head.v_blocks[j]
        v_ext = jnp.concatenate([v_ref[:, c0:c0 + MXU_DIM], jnp.ones((BF16_ROWS, MXU_DIM), BF16)], axis=0)
        pv = jnp.dot(v_ext, p_buf[pl.ds(z_ld, MXU_DIM), :], preferred_element_type=F32)
        if j == 0:
            state["acc"] = pv
        else:
            alpha = jnp.exp2((state["m"] - m_new) * head.scale)
            state["acc"] = state["acc"] * alpha + pv
        state["m"] = m_new
        if j == len(head.k_blocks) - 1:
            acc = state["acc"]
            dv = acc.shape[0] - BF16_ROWS
            finish(h, acc[:dv] * (1.0 / acc[dv:dv + 1]))

    for i in range(len(items) + LOOKAHEAD):
        if i < len(items):
            issue_scores(i)
        if i >= LOOKAHEAD:
            consume(i - LOOKAHEAD)


def _attn_kernel(*refs, lam_init, has_cache):
    it = iter(refs)
    zero_ref = next(it)
    lam_ref, dnw_ref = next(it), next(it)
    qdt_ref, kd_ref, vdt_ref, qmt_ref, kc_ref, vmt_ref = (next(it) for _ in range(6))
    kds, vdts, kcs, vmts = [kd_ref], [vdt_ref], [kc_ref], [vmt_ref]
    if has_cache:
        kds.append(next(it)); vdts.append(next(it)); kcs.append(next(it)); vmts.append(next(it))
    yb_ref, yc_ref = next(it), next(it)
    s_ring = [next(it) for _ in range(LOOKAHEAD + 1)]
    p_ring = [next(it) for _ in range(2)]

    lv = lam_ref[0]
    lam = (jnp.exp(jnp.sum(lv[0:1] * lv[1:2], axis=-1, keepdims=True))
           - jnp.exp(jnp.sum(lv[2:3] * lv[3:4], axis=-1, keepdims=True)) + lam_init)

    def diff_rhs(hm):
        def fn():
            q_dt = qdt_ref[...]
            feat = lax.broadcasted_iota(jnp.int32, q_dt.shape, 0)
            return jnp.where((feat >= hm * DIFF_DK) & (feat < (hm + 1) * DIFF_DK), q_dt, jnp.zeros_like(q_dt))
        return fn

    def mla_rhs(feats):
        return lambda: qmt_ref[feats, :]

    heads = []
    for hm in range(2 * DIFF_HEADS):
        rows = slice((hm // 2) * DIFF_DV, (hm // 2 + 1) * DIFF_DV)
        heads.append(_Head(kds, diff_rhs(hm), [vt.at[rows, :] for vt in vdts], DIFF_SCALE * LOG2E))
    for hd in range(MLA_HEADS):
        feats = slice(hd * MLA_HEAD_PAD, (hd + 1) * MLA_HEAD_PAD)
        rows = slice(hd * MLA_V, (hd + 1) * MLA_V)
        heads.append(_Head([kc.at[:, feats] for kc in kcs], mla_rhs(feats), [vt.at[rows, :] for vt in vmts],
                           MLA_SCALE * LOG2E))

    outs = []

    def finish(t, o):
        outs.append(o)
        if t == 2 * DIFF_HEADS - 1:
            y_heads = []
            for hd in range(DIFF_HEADS):
                d = outs[2 * hd] - lam * outs[2 * hd + 1]
                ms = jnp.mean(d * d, axis=0, keepdims=True)
                y_heads.append(d * lax.rsqrt(ms + 1e-6))
            y_b = jnp.concatenate(y_heads, axis=0).T
            yb_ref[...] = (y_b * dnw_ref[0] * (1.0 - lam_init)).astype(BF16)
        if t == len(heads) - 1:
            yc_ref[...] = jnp.concatenate(outs[2 * DIFF_HEADS:], axis=0).T.astype(BF16)

    _run_heads(heads, finish, zero_ref, s_ring, p_ring)


def _attn_call(lamv, dnw, qdt, kd, vdt, qmt, kc, vmt, cache, l, *, n_batch, seq, tq, lam_init):
    t_per_seq = seq // tq
    has_cache = cache is not None

    def q_spec(width):
        return pl.BlockSpec((None, width, tq), lambda b, t, z: (b, 0, t))

    def out_spec(width):
        return pl.BlockSpec((tq, width), lambda b, t, z: (b * t_per_seq + t, 0))

    in_specs = [
        pl.BlockSpec((1, 4, DIFF_DK), lambda b, t, z: (l, 0, 0)),
        pl.BlockSpec((1, 1, DIFF_WIDTH), lambda b, t, z: (l, 0, 0)),
        q_spec(DIFF_QK_COLS),
        pl.BlockSpec((seq, DIFF_QK_COLS), lambda b, t, z: (b, 0)),
        pl.BlockSpec((None, DIFF_WIDTH, seq), lambda b, t, z: (b, 0, 0)),
        q_spec(MLA_QK_PAD),
        pl.BlockSpec((seq, MLA_QK_PAD), lambda b, t, z: (b, 0)),
        pl.BlockSpec((None, MLA_WIDTH, seq), lambda b, t, z: (b, 0, 0)),
    ]
    args = [lamv, dnw, qdt, kd, vdt, qmt, kc, vmt]
    past = 0
    if has_cache:
        past = cache[0].shape[2]
        in_specs += [
            pl.BlockSpec((None, None, past, DIFF_QK_COLS), lambda b, t, z: (b, l, 0, 0)),
            pl.BlockSpec((None, None, DIFF_WIDTH, past), lambda b, t, z: (b, l, 0, 0)),
            pl.BlockSpec((None, None, past, MLA_QK_PAD), lambda b, t, z: (b, l, 0, 0)),
            pl.BlockSpec((None, None, MLA_WIDTH, past), lambda b, t, z: (b, l, 0, 0)),
        ]
        args += list(cache)
    rows = n_batch * seq
    ring = (2 * MXU_DIM, tq)
    return pl.pallas_call(
        functools.partial(_attn_kernel, lam_init=lam_init, has_cache=has_cache),
        out_shape=(jax.ShapeDtypeStruct((rows, DIFF_WIDTH), BF16),
                   jax.ShapeDtypeStruct((rows, MLA_WIDTH), BF16)),
        grid_spec=pltpu.PrefetchScalarGridSpec(
            num_scalar_prefetch=1,
            grid=(n_batch, t_per_seq),
            in_specs=in_specs,
            out_specs=(out_spec(DIFF_WIDTH), out_spec(MLA_WIDTH)),
            scratch_shapes=[pltpu.VMEM(ring, F32)] * (LOOKAHEAD + 1) + [pltpu.VMEM(ring, BF16)] * 2),
        compiler_params=_cparams(2), name=f"attn_l{l}_{'s' if has_cache else 'p'}",
    )(jnp.zeros((2,), jnp.int32), *args)


def _tail_kernel(x_ref, u_ref, up_ref, un_ref, ab_ref, yb_ref, yc_ref, cw_ref, g1_ref, sh2_ref, sc2_ref, g2_ref,
                 ln1g_ref, ln1b_ref, ln2g_ref, ln2b_ref, w_out_ref, w1_ref, w3_ref, w2_ref, o_ref, *, seq):
    tm = x_ref.shape[0]
    u = u_ref[...]
    row = lax.broadcasted_iota(jnp.int32, (tm, 1), 0)
    pos = (row + pl.program_id(0) * tm) % seq
    u_prev = jnp.where(row == 0, up_ref[SUBLANES - 1:SUBLANES, :], pltpu.roll(u, 1, axis=0))
    u_prev = jnp.where(pos == 0, 0.0, u_prev)
    u_next = jnp.where(row == tm - 1, un_ref[0:1, :], pltpu.roll(u, tm - 1, axis=0))
    u_next = jnp.where(pos == seq - 1, 0.0, u_next)
    cw = cw_ref[0]
    y_a = (ab_ref[...] * (u_prev * cw[0:1] + u * cw[1:2] + u_next * cw[2:3])).astype(BF16)

    n_sub = tm // TAIL_SUB
    st = [dict(rows=slice(r * TAIL_SUB, (r + 1) * TAIL_SUB)) for r in range(n_sub)]

    def out_proj(d):
        y_cat = jnp.concatenate([y_a[d["rows"]], yb_ref[d["rows"], :], yc_ref[d["rows"], :]], axis=1)
        d["y"] = jnp.dot(y_cat, w_out_ref[...], preferred_element_type=F32)

    def norm1(d):
        d["x1"] = _layer_norm(DEEPNORM_ALPHA * x_ref[d["rows"], :] + g1_ref[0] * d.pop("y"),
                              ln1g_ref[0], ln1b_ref[0])
        d["hf"] = (d["x1"] * (1.0 + sc2_ref[0]) + sh2_ref[0]).astype(BF16)
        d["f"] = None

    def ffn_chunk(c0, c1):
        def stage(d):
            g = jnp.dot(d["hf"], w1_ref[:, c0:c1], preferred_element_type=F32)
            a = (g * _sigmoid(g) * jnp.dot(d["hf"], w3_ref[:, c0:c1], preferred_element_type=F32)).astype(BF16)
            t = jnp.dot(a, w2_ref[c0:c1, :], preferred_element_type=F32)
            d["f"] = t if d["f"] is None else d["f"] + t
        return stage

    def norm2(d):
        o_ref[d["rows"], :] = _layer_norm(DEEPNORM_ALPHA * d.pop("x1") + g2_ref[0] * d.pop("f"),
                                          ln2g_ref[0], ln2b_ref[0])

    stages = [out_proj, norm1] + [ffn_chunk(c0, c1) for c0, c1 in FF_CHUNKS] + [norm2]
    for step in range(len(stages) + n_sub - 1):
        for r in range(n_sub):
            if 0 <= step - r < len(stages):
                stages[step - r](st[r])


def _tail_call(x2d, u, ab, yb, yc, mod3, wts, l, *, seq, tm, mod_row):
    rows = x2d.shape[0]
    n_tiles = rows // tm
    n_halo = rows // SUBLANES

    def mod_spec(j):
        return pl.BlockSpec((1, 1, D_MODEL), lambda i: ((l * COND_ROWS + mod_row(i)) * N_MOD + j, 0, 0))

    def row_spec(width):
        return pl.BlockSpec((tm, width), lambda i: (i, 0))

    in_specs = [
        row_spec(D_MODEL), row_spec(CONV_W),
        pl.BlockSpec((SUBLANES, CONV_W), lambda i: (jnp.maximum(i * (tm // SUBLANES) - 1, 0), 0)),
        pl.BlockSpec((SUBLANES, CONV_W), lambda i: (jnp.minimum((i + 1) * (tm // SUBLANES), n_halo - 1), 0)),
        row_spec(CONV_W), row_spec(DIFF_WIDTH), row_spec(MLA_WIDTH),
        pl.BlockSpec((1, 3, CONV_W), lambda i: (l, 0, 0)),
        mod_spec(2), mod_spec(3), mod_spec(4), mod_spec(5),
        _layer_spec((1, D_MODEL), l), _layer_spec((1, D_MODEL), l),
        _layer_spec((1, D_MODEL), l), _layer_spec((1, D_MODEL), l),
        _layer_spec((D_MODEL, D_MODEL), l),
        _layer_spec((D_MODEL, D_FF), l), _layer_spec((D_MODEL, D_FF), l), _layer_spec((D_FF, D_MODEL), l),
    ]
    return pl.pallas_call(
        functools.partial(_tail_kernel, seq=seq),
        out_shape=jax.ShapeDtypeStruct((rows, D_MODEL), F32),
        grid=(n_tiles,),
        in_specs=in_specs,
        out_specs=row_spec(D_MODEL),
        compiler_params=_cparams(1), name=f"tail_l{l}",
    )(x2d, u, u, u, ab, yb, yc, wts["conv_w"], mod3, mod3, mod3, mod3,
      wts["ln1_g"], wts["ln1_b"], wts["ln2_g"], wts["ln2_b"],
      wts["w_out"], wts["w_ff1"], wts["w_ff3"], wts["w_ff2"])


def _rope_tables(n_tokens):
    rows = n_tokens // GRID_W
    row = np.repeat(np.arange(rows), GRID_W).astype(np.float32)
    col = np.tile(np.arange(GRID_W), rows).astype(np.float32)
    half = MLA_ROPE // 4
    freqs = (np.float32(ROPE_BASE) ** (-np.arange(half, dtype=np.float32) / np.float32(half))).astype(np.float32)
    ang_r = row[:, None] * freqs[None, :]
    ang_c = col[:, None] * freqs[None, :]
    cos32 = np.concatenate([np.cos(ang_r), np.cos(ang_r), np.cos(ang_c), np.cos(ang_c)], axis=1)
    sin32 = np.concatenate([-np.sin(ang_r), np.sin(ang_r), -np.sin(ang_c), np.sin(ang_c)], axis=1)
    one = np.ones((n_tokens, 1), np.float32)
    zero = np.zeros((n_tokens, 1), np.float32)
    cos_d, sin_d = np.tile(cos32, (1, 4)), np.tile(sin32, (1, 4))
    cos_m = np.concatenate([np.tile(one, (1, MLA_NOPE)), cos32, np.tile(one, (1, 32))], axis=1)
    sin_m = np.concatenate([np.tile(zero, (1, MLA_NOPE)), sin32, np.tile(zero, (1, 32))], axis=1)
    cos_p = np.concatenate([cos32, np.tile(one, (1, 96))], axis=1)
    sin_p = np.concatenate([sin32, np.tile(zero, (1, 96))], axis=1)
    return tuple(jnp.asarray(t.astype(np.float32)) for t in (cos_d, sin_d, cos_m, sin_m, cos_p, sin_p))


def _prep_weights(w_in, w_uq, w_ukv, w_out, w_ff1, w_ff3, w_ff2, conv_w, q_norm_w, kv_norm_w,
                  ln1_g, ln1_b, ln2_g, ln2_b):
    n_l = w_in.shape[0]
    w_in_p = jnp.pad(w_in.astype(BF16), ((0, 0), (0, 0), (0, IN_COLS_PAD - IN_COLS)))
    w_uq_p = jnp.pad(w_uq.reshape(n_l, MLA_Q_RANK, MLA_HEADS, MLA_QK),
                     ((0, 0), (0, 0), (0, 0), (0, MLA_HEAD_PAD - MLA_QK))).reshape(n_l, MLA_Q_RANK, MLA_QK_PAD)
    w_ukv4 = w_ukv.reshape(n_l, MLA_KV_RANK, MLA_HEADS, MLA_NOPE + MLA_V)
    w_uk_p = jnp.pad(w_ukv4[..., :MLA_NOPE], ((0, 0), (0, 0), (0, 0), (0, MLA_HEAD_PAD - MLA_NOPE)))
    w_kv = jnp.concatenate([w_uk_p.reshape(n_l, MLA_KV_RANK, MLA_QK_PAD),
                            w_ukv4[..., MLA_NOPE:].reshape(n_l, MLA_KV_RANK, MLA_WIDTH)], axis=-1)
    vec = lambda a: a.reshape(n_l, 1, a.shape[-1])
    return {
        "w_in": w_in_p, "w_uq": w_uq_p.astype(BF16), "w_kv": w_kv.astype(BF16),
        "w_out": w_out.astype(BF16), "w_ff1": w_ff1.astype(BF16), "w_ff3": w_ff3.astype(BF16),
        "w_ff2": w_ff2.astype(BF16), "conv_w": conv_w,
        "q_norm_w": vec(q_norm_w), "kv_norm_w": vec(kv_norm_w),
        "ln1_g": vec(ln1_g), "ln1_b": vec(ln1_b), "ln2_g": vec(ln2_g), "ln2_b": vec(ln2_b),
    }


def kernel(x_prompt, x_sample, cache_diff_k, cache_diff_v, cache_mla_ckv, cache_mla_kpe, c, c_ctx,
           w_ada, b_ada, w_in, conv_w, lam_q1, lam_k1, lam_q2, lam_k2, diff_norm_w, q_norm_w, w_uq,
           kv_norm_w, w_ukv, w_out, ln1_g, ln1_b, w_ff1, w_ff3, w_ff2, ln2_g, ln2_b):
    n_l = w_in.shape[0]
    pb, ps, _ = x_prompt.shape
    sb, ss, _ = x_sample.shape
    past = cache_mla_ckv.shape[2]

    wts = _prep_weights(w_in, w_uq, w_ukv, w_out, w_ff1, w_ff3, w_ff2, conv_w, q_norm_w, kv_norm_w,
                        ln1_g, ln1_b, ln2_g, ln2_b)
    lamv = jnp.stack([lam_q1, lam_k1, lam_q2, lam_k2], axis=1)
    dnw = jnp.tile(diff_norm_w, (1, DIFF_HEADS)).reshape(n_l, 1, DIFF_WIDTH)

    cond = jnp.concatenate([c_ctx[None, :], c, jnp.zeros((COND_ROWS - 1 - sb, D_MODEL), F32)], axis=0)
    mod3 = _ada_call(cond, w_ada, b_ada).reshape(n_l * COND_ROWS * N_MOD, 1, D_MODEL)

    ck = jnp.transpose(cache_diff_k, (0, 1, 4, 2, 3, 5)).reshape(sb, n_l, past, DIFF_QK_COLS).astype(BF16)
    cv_t = jnp.swapaxes(cache_diff_v, -1, -2).reshape(sb, n_l, DIFF_WIDTH, past).astype(BF16)
    kpe_slab = jnp.pad(cache_mla_kpe, ((0, 0), (0, 0), (0, 0), (0, LANES - MLA_ROPE)))
    ckc, cvm_t = _cache_kv_call(cache_mla_ckv, kpe_slab, wts["w_kv"])

    rope_tabs = _rope_tables(ss)
    tm_in_p, tm_in, tm_tail, tq_s = 512, 1024, 512, 512

    def sample_row(tm):
        return lambda i: 1 + i // (ss // tm)

    xp = x_prompt.reshape(pb * ps, D_MODEL)
    xs = x_sample.reshape(sb * ss, D_MODEL)
    states = ()
    for l in range(n_l):
        lam_init = 0.8 - 0.6 * math.exp(-0.3 * l)

        row_p = lambda i: 0
        outs = _inproj_call(xp, mod3, wts, l, n_batch=pb, seq=ps, tm=tm_in_p, mod_row=row_p, rope_tabs=None,
                            n_layers=n_l, states=states)
        u, ab, qdt, kd, vdt, qmt, kc, vmt = outs[:8]
        states = tuple(outs[8:])
        yb, yc = _attn_call(lamv, dnw, qdt, kd, vdt, qmt, kc, vmt, None, l,
                            n_batch=pb, seq=ps, tq=ps, lam_init=lam_init)
        xp = _tail_call(xp, u, ab, yb, yc, mod3, wts, l, seq=ps, tm=tm_tail, mod_row=row_p)

        u, ab, qdt, kd, vdt, qmt, kc, vmt = _inproj_call(
            xs, mod3, wts, l, n_batch=sb, seq=ss, tm=tm_in, mod_row=sample_row(tm_in), rope_tabs=rope_tabs,
            n_layers=n_l, states=None)
        yb, yc = _attn_call(lamv, dnw, qdt, kd, vdt, qmt, kc, vmt, (ck, cv_t, ckc, cvm_t), l,
                            n_batch=sb, seq=ss, tq=tq_s, lam_init=lam_init)
        xs = _tail_call(xs, u, ab, yb, yc, mod3, wts, l, seq=ss, tm=tm_tail, mod_row=sample_row(tm_tail))

    return (xp.reshape(pb, ps, D_MODEL), xs.reshape(sb, ss, D_MODEL)) + states
```

```python
import functools
import math

import numpy as np
import jax
import jax.numpy as jnp
from jax import lax
from jax.experimental import pallas as pl
from jax.experimental.pallas import tpu as pltpu

F32 = jnp.float32
BF16 = jnp.bfloat16

D_MODEL = 1024
DEPTH = 2
GRID_W = 64
ROPE_BASE = 10000.0
CONV_W = 256
DIFF_HEADS = 4
DIFF_DK = 32
DIFF_DV = 64
DIFF_WIDTH = DIFF_HEADS * DIFF_DV
DIFF_QK_COLS = DIFF_HEADS * 2 * DIFF_DK
DIFF_SCALE = DIFF_DK ** -0.5
MLA_HEADS = 8
MLA_Q_RANK = 384
MLA_KV_RANK = 256
MLA_NOPE = 64
MLA_ROPE = 32
MLA_V = 64
MLA_QK = MLA_NOPE + MLA_ROPE
MLA_WIDTH = MLA_HEADS * MLA_V
MLA_SCALE = MLA_QK ** -0.5
IN_COLS = 3 * CONV_W + 2 * DIFF_QK_COLS + DIFF_WIDTH + MLA_Q_RANK + MLA_KV_RANK + MLA_ROPE
D_FF = 2816
DEEPNORM_ALPHA = (2 * DEPTH) ** 0.25
LOG2E = 1.4426950408889634

LANES = 128
SUBLANES = 8
BF16_ROWS = 16
MXU_DIM = 256
IN_COLS_PAD = 2304
MLA_HEAD_PAD = LANES
MLA_QK_PAD = MLA_HEADS * MLA_HEAD_PAD
N_MOD = 6
COND_ROWS = 8
FF_CHUNKS = ((0, 1536), (1536, 2816))
KEY_CHUNK = 64
LOOKAHEAD = 3
TAIL_SUB = 256
INPROJ_SUB = 256
VMEM_LIMIT = 56 * 2 ** 20

OFF_AX, OFF_AB, OFF_AC = 0, 256, 512
OFF_DQ, OFF_DK, OFF_DV = 768, 1024, 1280
OFF_CQ, OFF_CKV, OFF_KPE = 1536, 1920, 2176


def _cparams(n_grid):
    return pltpu.CompilerParams(dimension_semantics=("arbitrary",) * n_grid,
                                vmem_limit_bytes=VMEM_LIMIT)


def _layer_spec(tail, l):
    nd = len(tail)
    return pl.BlockSpec((None,) + tuple(tail), lambda *_: (l,) + (0,) * nd,
                        pipeline_mode=pl.Buffered(1))


def _sigmoid(x):
    return 1.0 / (1.0 + jnp.exp(-x))


def _layer_norm(x, g, b):
    mu = jnp.mean(x, axis=-1, keepdims=True)
    xc = x - mu
    var = jnp.mean(xc * xc, axis=-1, keepdims=True)
    return xc * lax.rsqrt(var + 1e-5) * g + b


def _rms_norm(x, w):
    ms = jnp.mean(x * x, axis=-1, keepdims=True)
    return x * lax.rsqrt(ms + 1e-6) * w


def _rope_slab(x, cos, sin_signed):
    lane = lax.broadcasted_iota(jnp.int32, x.shape, 1)
    lo = (lane % 16) < 8
    partner = jnp.where(lo, pltpu.roll(x, LANES - 8, axis=1), pltpu.roll(x, 8, axis=1))
    return x * cos + partner * sin_signed


def _rope_window(cos, sin_signed, lane0):
    lane = lax.broadcasted_iota(jnp.int32, cos.shape, 1)
    inside = (lane >= lane0) & (lane < lane0 + MLA_ROPE)
    return jnp.where(inside, cos, 1.0), jnp.where(inside, sin_signed, 0.0)


def _ada_kernel(cond_ref, w_ref, b_ref, o_ref):
    c = cond_ref[...]
    s = (c * _sigmoid(c)).astype(BF16)
    o_ref[0] = jnp.dot(s, w_ref[0].astype(BF16), preferred_element_type=F32) + b_ref[0]


def _ada_call(cond, w_ada, b_ada):
    n_l = w_ada.shape[0]
    tn = 2 * D_MODEL
    return pl.pallas_call(
        _ada_kernel,
        out_shape=jax.ShapeDtypeStruct((n_l, COND_ROWS, N_MOD * D_MODEL), F32),
        grid=(n_l, N_MOD * D_MODEL // tn),
        in_specs=[
            pl.BlockSpec((COND_ROWS, D_MODEL), lambda l, j: (0, 0)),
            pl.BlockSpec((1, D_MODEL, tn), lambda l, j: (l, 0, j)),
            pl.BlockSpec((1, 1, tn), lambda l, j: (l, 0, j)),
        ],
        out_specs=pl.BlockSpec((1, COND_ROWS, tn), lambda l, j: (l, 0, j)),
        compiler_params=_cparams(2),
        name="ada_mod",
    )(cond, w_ada, b_ada.reshape(n_l, 1, N_MOD * D_MODEL))


def _expand_kv(ckv, kpe_slab, w_kv_ref):
    kv = jnp.dot(ckv.astype(BF16), w_kv_ref[...], preferred_element_type=F32)
    kpe_shift = pltpu.roll(kpe_slab, MLA_NOPE, axis=1)
    k_cat = jnp.concatenate(
        [(kv[:, h * MLA_HEAD_PAD:(h + 1) * MLA_HEAD_PAD] + kpe_shift).astype(BF16) for h in range(MLA_HEADS)],
        axis=1)
    return k_cat, kv[:, MLA_QK_PAD:MLA_QK_PAD + MLA_WIDTH]


def _cache_kv_kernel(ckv_ref, kpe_ref, w_kv_ref, kc_ref, vmt_ref):
    k_cat, v_m = _expand_kv(ckv_ref[...], kpe_ref[...], w_kv_ref)
    kc_ref[...] = k_cat
    vmt_ref[...] = v_m.T.astype(BF16)


def _cache_kv_call(cache_ckv, cache_kpe_slab, w_kv):
    n_b, n_l, past, _ = cache_ckv.shape
    return pl.pallas_call(
        _cache_kv_kernel,
        out_shape=(jax.ShapeDtypeStruct((n_b, n_l, past, MLA_QK_PAD), BF16),
                   jax.ShapeDtypeStruct((n_b, n_l, MLA_WIDTH, past), BF16)),
        grid=(n_b, n_l),
        in_specs=[
            pl.BlockSpec((None, None, past, MLA_KV_RANK), lambda b, l: (b, l, 0, 0)),
            pl.BlockSpec((None, None, past, LANES), lambda b, l: (b, l, 0, 0)),
            pl.BlockSpec((None, MLA_KV_RANK, MLA_QK_PAD + MLA_WIDTH), lambda b, l: (l, 0, 0)),
        ],
        out_specs=(pl.BlockSpec((None, None, past, MLA_QK_PAD), lambda b, l: (b, l, 0, 0)),
                   pl.BlockSpec((None, None, MLA_WIDTH, past), lambda b, l: (b, l, 0, 0))),
        compiler_params=_cparams(2),
        name="cache_kv",
    )(cache_ckv, cache_kpe_slab, w_kv)


N_STATE = 4


def _inproj_kernel(*refs, rope, n_seq, seq, with_state, alias_state):
    it = iter(refs)
    x_ref, sc_ref, sh_ref = next(it), next(it), next(it)
    w_in_ref, qnw_ref, w_uq_ref, kvnw_ref, w_kv_ref = next(it), next(it), next(it), next(it), next(it)
    if rope:
        cd_ref, sd_ref = next(it), next(it)
    if alias_state:
        for _ in range(N_STATE):
            next(it)
    u_ref, ab_ref, qdt_ref, kd_ref, vdt_ref, qmt_ref, kc_ref, vmt_ref = (next(it) for _ in range(8))
    if with_state:
        stk_ref, stv_ref, stc_ref, stp_ref = (next(it) for _ in range(N_STATE))

    tm = x_ref.shape[0]
    sub = seq if n_seq else INPROJ_SUB
    st = [dict(r=r, rows=slice(r * sub, (r + 1) * sub)) for r in range(tm // sub)]

    def project(d):
        h = (x_ref[d["rows"], :] * (1.0 + sc_ref[0]) + sh_ref[0]).astype(BF16)
        d["proj"] = jnp.dot(h, w_in_ref[...], preferred_element_type=F32)

    def split(d):
        proj, rows = d.pop("proj"), d["rows"]
        u_ref[rows, :] = proj[:, OFF_AC:OFF_AC + CONV_W] * proj[:, OFF_AX:OFF_AX + CONV_W]
        ab_ref[rows, :] = proj[:, OFF_AB:OFF_AB + CONV_W]
        q = proj[:, OFF_DQ:OFF_DQ + DIFF_QK_COLS]
        k = proj[:, OFF_DK:OFF_DK + DIFF_QK_COLS]
        d["v"] = proj[:, OFF_DV:OFF_DV + DIFF_WIDTH]
        if rope:
            cd, sd = cd_ref[rows, :], sd_ref[rows, :]
            q = jnp.concatenate([_rope_slab(q[:, j * LANES:(j + 1) * LANES], cd, sd) for j in range(2)], axis=1)
            k = jnp.concatenate([_rope_slab(k[:, j * LANES:(j + 1) * LANES], cd, sd) for j in range(2)], axis=1)
        kd_ref[rows, :] = k.astype(BF16)
        d["q"], d["k"] = q, k
        d["cq"] = _rms_norm(proj[:, OFF_CQ:OFF_CQ + MLA_Q_RANK], qnw_ref[0]).astype(BF16)
        d["ckv"] = _rms_norm(proj[:, OFF_CKV:OFF_CKV + MLA_KV_RANK], kvnw_ref[0])
        kpe_slab = proj[:, OFF_KPE:OFF_KPE + LANES]
        if rope:
            kpe_slab = _rope_slab(kpe_slab, *_rope_window(cd_ref[rows, :], sd_ref[rows, :], 0))
        d["kpe"] = kpe_slab

    def up_project(d):
        d["qc"] = jnp.dot(d.pop("cq"), w_uq_ref[...], preferred_element_type=F32)
        d["k_cat"], d["v_m"] = _expand_kv(d["ckv"], d["kpe"], w_kv_ref)

    def emit(d):
        r, rows = d["r"], d["rows"]
        qc = d.pop("qc")
        if rope:
            cm, sm = _rope_window(cd_ref[rows, :], sd_ref[rows, :], MLA_NOPE)
            qc = jnp.concatenate(
                [_rope_slab(qc[:, hd * MLA_HEAD_PAD:(hd + 1) * MLA_HEAD_PAD], cm, sm) for hd in range(MLA_HEADS)],
                axis=1)
        kc_ref[rows, :] = d.pop("k_cat")
        for dst, val in ((qdt_ref, d["q"]), (vdt_ref, d["v"]), (qmt_ref, qc), (vmt_ref, d.pop("v_m"))):
            val_t = val.T.astype(BF16)
            if n_seq:
                dst[r] = val_t
            else:
                dst[:, rows] = val_t
        if with_state:
            fills = [()] if alias_state else [(ll,) for ll in range(stc_ref.shape[1])]
            for ll in fills:
                stc_ref[(r,) + ll] = d["ckv"]
                stp_ref[(r,) + ll] = d["kpe"][:, :MLA_ROPE]
                for hm in range(2 * DIFF_HEADS):
                    stk_ref[(r,) + ll + (hm // 2, hm % 2)] = d["k"][:, hm * DIFF_DK:(hm + 1) * DIFF_DK]
                for hd in range(DIFF_HEADS):
                    stv_ref[(r,) + ll + (hd,)] = d["v"][:, hd * DIFF_DV:(hd + 1) * DIFF_DV]

    stages = [project, split, up_project, emit]
    for step in range(len(stages) + len(st) - 1):
        for r, d in enumerate(st):
            if 0 <= step - r < len(stages):
                stages[step - r](d)


def _inproj_call(x2d, mod3, wts, l, *, n_batch, seq, tm, mod_row, rope_tabs, n_layers, states):
    rows = n_batch * seq
    n_tiles = rows // tm
    rope = rope_tabs is not None
    with_state = states is not None
    alias_state = bool(states)
    n_seq = tm // seq
    t_per_seq = max(seq // tm, 1)

    def mod_spec(j):
        return pl.BlockSpec((1, 1, D_MODEL), lambda i: ((l * COND_ROWS + mod_row(i)) * N_MOD + j, 0, 0))

    def row_spec(width):
        return pl.BlockSpec((tm, width), lambda i: (i, 0))

    def feat_spec(width):
        if n_seq:
            return pl.BlockSpec((n_seq, width, seq), lambda i: (i, 0, 0))
        return pl.BlockSpec((None, width, tm), lambda i: (i // t_per_seq, 0, i % t_per_seq))

    in_specs = [
        row_spec(D_MODEL), mod_spec(1), mod_spec(0),
        _layer_spec((D_MODEL, IN_COLS_PAD), l),
        _layer_spec((1, MLA_Q_RANK), l),
        _layer_spec((MLA_Q_RANK, MLA_QK_PAD), l),
        _layer_spec((1, MLA_KV_RANK), l),
        _layer_spec((MLA_KV_RANK, MLA_QK_PAD + MLA_WIDTH), l),
    ]
    args = [x2d, mod3, mod3, wts["w_in"], wts["q_norm_w"], wts["w_uq"], wts["kv_norm_w"], wts["w_kv"]]
    if rope:
        in_specs += [pl.BlockSpec((tm, LANES), lambda i: (i % t_per_seq, 0))] * 2
        args += list(rope_tabs)

    out_shape = [
        jax.ShapeDtypeStruct((rows, CONV_W), F32),
        jax.ShapeDtypeStruct((rows, CONV_W), F32),
        jax.ShapeDtypeStruct((n_batch, DIFF_QK_COLS, seq), BF16),
        jax.ShapeDtypeStruct((rows, DIFF_QK_COLS), BF16),
        jax.ShapeDtypeStruct((n_batch, DIFF_WIDTH, seq), BF16),
        jax.ShapeDtypeStruct((n_batch, MLA_QK_PAD, seq), BF16),
        jax.ShapeDtypeStruct((rows, MLA_QK_PAD), BF16),
        jax.ShapeDtypeStruct((n_batch, MLA_WIDTH, seq), BF16),
    ]
    out_specs = [row_spec(CONV_W), row_spec(CONV_W), feat_spec(DIFF_QK_COLS), row_spec(DIFF_QK_COLS),
                 feat_spec(DIFF_WIDTH), feat_spec(MLA_QK_PAD), row_spec(MLA_QK_PAD), feat_spec(MLA_WIDTH)]
    aliases = {}
    if with_state:
        assert n_seq >= 1
        n_out = len(out_shape)
        out_shape += [
            jax.ShapeDtypeStruct((n_batch, n_layers, DIFF_HEADS, 2, seq, DIFF_DK), F32),
            jax.ShapeDtypeStruct((n_batch, n_layers, DIFF_HEADS, seq, DIFF_DV), F32),
            jax.ShapeDtypeStruct((n_batch, n_layers, seq, MLA_KV_RANK), F32),
            jax.ShapeDtypeStruct((n_batch, n_layers, seq, MLA_ROPE), F32),
        ]
        lyr, l_idx = (None, l) if alias_state else (n_layers, 0)
        out_specs += [
            pl.BlockSpec((n_seq, lyr, DIFF_HEADS, 2, seq, DIFF_DK), lambda i: (i, l_idx, 0, 0, 0, 0)),
            pl.BlockSpec((n_seq, lyr, DIFF_HEADS, seq, DIFF_DV), lambda i: (i, l_idx, 0, 0, 0)),
            pl.BlockSpec((n_seq, lyr, seq, MLA_KV_RANK), lambda i: (i, l_idx, 0, 0)),
            pl.BlockSpec((n_seq, lyr, seq, MLA_ROPE), lambda i: (i, l_idx, 0, 0)),
        ]
        if alias_state:
            for j, st in enumerate(states):
                aliases[len(args)] = n_out + j
                in_specs.append(pl.BlockSpec(memory_space=pl.ANY))
                args.append(st)

    kern = functools.partial(_inproj_kernel, rope=rope, n_seq=n_seq, seq=seq, with_state=with_state,
                             alias_state=alias_state)
    return pl.pallas_call(
        kern, out_shape=tuple(out_shape), grid=(n_tiles,),
        in_specs=in_specs, out_specs=tuple(out_specs), input_output_aliases=aliases,
        compiler_params=_cparams(1), name=f"inproj_l{l}_{'s' if rope else 'p'}",
    )(*args)


def _tree(op, xs):
    xs = list(xs)
    while len(xs) > 1:
        xs = [op(xs[i], xs[i + 1]) if i + 1 < len(xs) else xs[i] for i in range(0, len(xs), 2)]
    return xs[0]


def _slabs(x):
    return [x[j * SUBLANES:(j + 1) * SUBLANES] for j in range(x.shape[0] // SUBLANES)]


class _Head:
    def __init__(self, k_refs, rhs_fn, vt_refs, scale_log2e):
        self.k_blocks = [(k, r0) for k in k_refs for r0 in range(0, k.shape[0], MXU_DIM)]
        self.v_blocks = [(v, c0) for v in vt_refs for c0 in range(0, v.shape[1], MXU_DIM)]
        self.rhs_fn, self.scale = rhs_fn, scale_log2e


def _run_heads(heads, finish, zero_ref, s_ring, p_ring):
    items = [(h, j) for h in range(len(heads)) for j in range(len(heads[h].k_blocks))]
    z_st = pl.multiple_of(zero_ref[0], MXU_DIM)
    z_ld = pl.multiple_of(zero_ref[1], MXU_DIM)
    rhs = {}
    blk_max = {}
    state = {}

    def issue_scores(i):
        h, j = items[i]
        if j == 0:
            rhs[h] = heads[h].rhs_fn()
        k_ref, r0 = heads[h].k_blocks[j]
        s = jnp.dot(k_ref[r0:r0 + MXU_DIM, :], rhs[h], preferred_element_type=F32)
        s_ring[i % len(s_ring)][pl.ds(z_st, MXU_DIM), :] = s
        blk_max[i] = _tree(jnp.maximum, _slabs(s))

    def consume(i):
        h, j = items[i]
        head = heads[h]
        s_buf, p_buf = s_ring[i % len(s_ring)], p_ring[i % len(p_ring)]
        m_blk = jnp.max(blk_max.pop(i), axis=0, keepdims=True)
        m_new = m_blk if j == 0 else jnp.maximum(state["m"], m_blk)
        for c0 in range(0, MXU_DIM, KEY_CHUNK):
            p = jnp.exp2((s_buf[pl.ds(z_ld + c0, KEY_CHUNK), :] - m_new) * head.scale)
            p_buf[pl.ds(z_st + c0, KEY_CHUNK), :] = p.astype(BF16)
        v_ref, c0 = head.v_blocks[j]
        v_ext = jnp.concatenate([v_ref[:, c0:c0 + MXU_DIM], jnp.ones((BF16_ROWS, MXU_DIM), BF16)], axis=0)
        pv = jnp.dot(v_ext, p_buf[pl.ds(z_ld, MXU_DIM), :], preferred_element_type=F32)
        if j == 0:
            state["acc"] = pv
        else:
            alpha = jnp.exp2((state["m"] - m_new) * head.scale)
            state["acc"] = state["acc"] * alpha + pv
        state["m"] = m_new
        if j == len(head.k_blocks) - 1:
            acc = state["acc"]
            dv = acc.shape[0] - BF16_ROWS
            finish(h, acc[:dv] * (1.0 / acc[dv:dv + 1]))

    for i in range(len(items) + LOOKAHEAD):
        if i < len(items):
            issue_scores(i)
        if i >= LOOKAHEAD:
            consume(i - LOOKAHEAD)


def _attn_kernel(*refs, lam_init, has_cache):
    it = iter(refs)
    zero_ref = next(it)
    lam_ref, dnw_ref = next(it), next(it)
    qdt_ref, kd_ref, vdt_ref, qmt_ref, kc_ref, vmt_ref = (next(it) for _ in range(6))
    kds, vdts, kcs, vmts = [kd_ref], [vdt_ref], [kc_ref], [vmt_ref]
    if has_cache:
        kds.append(next(it)); vdts.append(next(it)); kcs.append(next(it)); vmts.append(next(it))
    yb_ref, yc_ref = next(it), next(it)
    s_ring = [next(it) for _ in range(LOOKAHEAD + 1)]
    p_ring = [next(it) for _ in range(2)]

    lv = lam_ref[0]
    lam = (jnp.exp(jnp.sum(lv[0:1] * lv[1:2], axis=-1, keepdims=True))
           - jnp.exp(jnp.sum(lv[2:3] * lv[3:4], axis=-1, keepdims=True)) + lam_init)

    def diff_rhs(hm):
        def fn():
            q_dt = qdt_ref[...]
            feat = lax.broadcasted_iota(jnp.int32, q_dt.shape, 0)
            return jnp.where((feat >= hm * DIFF_DK) & (feat < (hm + 1) * DIFF_DK), q_dt, jnp.zeros_like(q_dt))
        return fn

    def mla_rhs(feats):
        return lambda: qmt_ref[feats, :]

    heads = []
    for hm in range(2 * DIFF_HEADS):
        rows = slice((hm // 2) * DIFF_DV, (hm // 2 + 1) * DIFF_DV)
        heads.append(_Head(kds, diff_rhs(hm), [vt.at[rows, :] for vt in vdts], DIFF_SCALE * LOG2E))
    for hd in range(MLA_HEADS):
        feats = slice(hd * MLA_HEAD_PAD, (hd + 1) * MLA_HEAD_PAD)
        rows = slice(hd * MLA_V, (hd + 1) * MLA_V)
        heads.append(_Head([kc.at[:, feats] for kc in kcs], mla_rhs(feats), [vt.at[rows, :] for vt in vmts],
                           MLA_SCALE * LOG2E))

    outs = []

    def finish(t, o):
        outs.append(o)
        if t == 2 * DIFF_HEADS - 1:
            y_heads = []
            for hd in range(DIFF_HEADS):
                d = outs[2 * hd] - lam * outs[2 * hd + 1]
                ms = jnp.mean(d * d, axis=0, keepdims=True)
                y_heads.append(d * lax.rsqrt(ms + 1e-6))
            y_b = jnp.concatenate(y_heads, axis=0).T
            yb_ref[...] = (y_b * dnw_ref[0] * (1.0 - lam_init)).astype(BF16)
        if t == len(heads) - 1:
            yc_ref[...] = jnp.concatenate(outs[2 * DIFF_HEADS:], axis=0).T.astype(BF16)

    _run_heads(heads, finish, zero_ref, s_ring, p_ring)


def _attn_call(lamv, dnw, qdt, kd, vdt, qmt, kc, vmt, cache, l, *, n_batch, seq, tq, lam_init):
    t_per_seq = seq // tq
    has_cache = cache is not None

    def q_spec(width):
        return pl.BlockSpec((None, width, tq), lambda b, t, z: (b, 0, t))

    def out_spec(width):
        return pl.BlockSpec((tq, width), lambda b, t, z: (b * t_per_seq + t, 0))

    in_specs = [
        pl.BlockSpec((1, 4, DIFF_DK), lambda b, t, z: (l, 0, 0)),
        pl.BlockSpec((1, 1, DIFF_WIDTH), lambda b, t, z: (l, 0, 0)),
        q_spec(DIFF_QK_COLS),
        pl.BlockSpec((seq, DIFF_QK_COLS), lambda b, t, z: (b, 0)),
        pl.BlockSpec((None, DIFF_WIDTH, seq), lambda b, t, z: (b, 0, 0)),
        q_spec(MLA_QK_PAD),
        pl.BlockSpec((seq, MLA_QK_PAD), lambda b, t, z: (b, 0)),
        pl.BlockSpec((None, MLA_WIDTH, seq), lambda b, t, z: (b, 0, 0)),
    ]
    args = [lamv, dnw, qdt, kd, vdt, qmt, kc, vmt]
    past = 0
    if has_cache:
        past = cache[0].shape[2]
        in_specs += [
            pl.BlockSpec((None, None, past, DIFF_QK_COLS), lambda b, t, z: (b, l, 0, 0)),
            pl.BlockSpec((None, None, DIFF_WIDTH, past), lambda b, t, z: (b, l, 0, 0)),
            pl.BlockSpec((None, None, past, MLA_QK_PAD), lambda b, t, z: (b, l, 0, 0)),
            pl.BlockSpec((None, None, MLA_WIDTH, past), lambda b, t, z: (b, l, 0, 0)),
        ]
        args += list(cache)
    rows = n_batch * seq
    ring = (2 * MXU_DIM, tq)
    return pl.pallas_call(
        functools.partial(_attn_kernel, lam_init=lam_init, has_cache=has_cache),
        out_shape=(jax.ShapeDtypeStruct((rows, DIFF_WIDTH), BF16),
                   jax.ShapeDtypeStruct((rows, MLA_WIDTH), BF16)),
        grid_spec=pltpu.PrefetchScalarGridSpec(
            num_scalar_prefetch=1,
            grid=(n_batch, t_per_seq),
            in_specs=in_specs,
            out_specs=(out_spec(DIFF_WIDTH), out_spec(MLA_WIDTH)),
            scratch_shapes=[pltpu.VMEM(ring, F32)] * (LOOKAHEAD + 1) + [pltpu.VMEM(ring, BF16)] * 2),
        compiler_params=_cparams(2), name=f"attn_l{l}_{'s' if has_cache else 'p'}",
    )(jnp.zeros((2,), jnp.int32), *args)


def _tail_kernel(x_ref, u_ref, up_ref, un_ref, ab_ref, yb_ref, yc_ref, cw_ref, g1_ref, sh2_ref, sc2_ref, g2_ref,
                 ln1g_ref, ln1b_ref, ln2g_ref, ln2b_ref, w_out_ref, w1_ref, w3_ref, w2_ref, o_ref, *, seq):
    tm = x_ref.shape[0]
    u = u_ref[...]
    row = lax.broadcasted_iota(jnp.int32, (tm, 1), 0)
    pos = (row + pl.program_id(0) * tm) % seq
    u_prev = jnp.where(row == 0, up_ref[SUBLANES - 1:SUBLANES, :], pltpu.roll(u, 1, axis=0))
    u_prev = jnp.where(pos == 0, 0.0, u_prev)
    u_next = jnp.where(row == tm - 1, un_ref[0:1, :], pltpu.roll(u, tm - 1, axis=0))
    u_next = jnp.where(pos == seq - 1, 0.0, u_next)
    cw = cw_ref[0]
    y_a = (ab_ref[...] * (u_prev * cw[0:1] + u * cw[1:2] + u_next * cw[2:3])).astype(BF16)

    n_sub = tm // TAIL_SUB
    st = [dict(rows=slice(r * TAIL_SUB, (r + 1) * TAIL_SUB)) for r in range(n_sub)]

    def out_proj(d):
        y_cat = jnp.concatenate([y_a[d["rows"]], yb_ref[d["rows"], :], yc_ref[d["rows"], :]], axis=1)
        d["y"] = jnp.dot(y_cat, w_out_ref[...], preferred_element_type=F32)

    def norm1(d):
        d["x1"] = _layer_norm(DEEPNORM_ALPHA * x_ref[d["rows"], :] + g1_ref[0] * d.pop("y"),
                              ln1g_ref[0], ln1b_ref[0])
        d["hf"] = (d["x1"] * (1.0 + sc2_ref[0]) + sh2_ref[0]).astype(BF16)
        d["f"] = None

    def ffn_chunk(c0, c1):
        def stage(d):
            g = jnp.dot(d["hf"], w1_ref[:, c0:c1], preferred_element_type=F32)
            a = (g * _sigmoid(g) * jnp.dot(d["hf"], w3_ref[:, c0:c1], preferred_element_type=F32)).astype(BF16)
            t = jnp.dot(a, w2_ref[c0:c1, :], preferred_element_type=F32)
            d["f"] = t if d["f"] is None else d["f"] + t
        return stage

    def norm2(d):
        o_ref[d["rows"], :] = _layer_norm(DEEPNORM_ALPHA * d.pop("x1") + g2_ref[0] * d.pop("f"),
                                          ln2g_ref[0], ln2b_ref[0])

    stages = [out_proj, norm1] + [ffn_chunk(c0, c1) for c0, c1 in FF_CHUNKS] + [norm2]
    for step in range(len(stages) + n_sub - 1):
        for r in range(n_sub):
            if 0 <= step - r < len(stages):
                stages[step - r](st[r])


def _tail_call(x2d, u, ab, yb, yc, mod3, wts, l, *, seq, tm, mod_row):
    rows = x2d.shape[0]
    n_tiles = rows // tm
    n_halo = rows // SUBLANES

    def mod_spec(j):
        return pl.BlockSpec((1, 1, D_MODEL), lambda i: ((l * COND_ROWS + mod_row(i)) * N_MOD + j, 0, 0))

    def row_spec(width):
        return pl.BlockSpec((tm, width), lambda i: (i, 0))

    in_specs = [
        row_spec(D_MODEL), row_spec(CONV_W),
        pl.BlockSpec((SUBLANES, CONV_W), lambda i: (jnp.maximum(i * (tm // SUBLANES) - 1, 0), 0)),
        pl.BlockSpec((SUBLANES, CONV_W), lambda i: (jnp.minimum((i + 1) * (tm // SUBLANES), n_halo - 1), 0)),
        row_spec(CONV_W), row_spec(DIFF_WIDTH), row_spec(MLA_WIDTH),
        pl.BlockSpec((1, 3, CONV_W), lambda i: (l, 0, 0)),
        mod_spec(2), mod_spec(3), mod_spec(4), mod_spec(5),
        _layer_spec((1, D_MODEL), l), _layer_spec((1, D_MODEL), l),
        _layer_spec((1, D_MODEL), l), _layer_spec((1, D_MODEL), l),
        _layer_spec((D_MODEL, D_MODEL), l),
        _layer_spec((D_MODEL, D_FF), l), _layer_spec((D_MODEL, D_FF), l), _layer_spec((D_FF, D_MODEL), l),
    ]
    return pl.pallas_call(
        functools.partial(_tail_kernel, seq=seq),
        out_shape=jax.ShapeDtypeStruct((rows, D_MODEL), F32),
        grid=(n_tiles,),
        in_specs=in_specs,
        out_specs=row_spec(D_MODEL),
        compiler_params=_cparams(1), name=f"tail_l{l}",
    )(x2d, u, u, u, ab, yb, yc, wts["conv_w"], mod3, mod3, mod3, mod3,
      wts["ln1_g"], wts["ln1_b"], wts["ln2_g"], wts["ln2_b"],
      wts["w_out"], wts["w_ff1"], wts["w_ff3"], wts["w_ff2"])


def _rope_tables(n_tokens):
    rows = n_tokens // GRID_W
    row = np.repeat(np.arange(rows), GRID_W).astype(np.float32)
    col = np.tile(np.arange(GRID_W), rows).astype(np.float32)
    half = MLA_ROPE // 4
    freqs = (np.float32(ROPE_BASE) ** (-np.arange(half, dtype=np.float32) / np.float32(half))).astype(np.float32)
    ang_r = row[:, None] * freqs[None, :]
    ang_c = col[:, None] * freqs[None, :]
    cos32 = np.concatenate([np.cos(ang_r), np.cos(ang_r), np.cos(ang_c), np.cos(ang_c)], axis=1)
    sin32 = np.concatenate([-np.sin(ang_r), np.sin(ang_r), -np.sin(ang_c), np.sin(ang_c)], axis=1)
    reps = LANES // cos32.shape[1]
    return tuple(jnp.asarray(np.tile(t, (1, reps)).astype(np.float32)) for t in (cos32, sin32))


def _prep_weights(w_in, w_uq, w_ukv, w_out, w_ff1, w_ff3, w_ff2, conv_w, q_norm_w, kv_norm_w,
                  ln1_g, ln1_b, ln2_g, ln2_b):
    n_l = w_in.shape[0]
    w_in_p = jnp.concatenate(
        [w_in.astype(BF16), jnp.zeros((n_l, D_MODEL, IN_COLS_PAD - IN_COLS), BF16)], axis=-1)
    w_uq_p = jnp.pad(w_uq.reshape(n_l, MLA_Q_RANK, MLA_HEADS, MLA_QK),
                     ((0, 0), (0, 0), (0, 0), (0, MLA_HEAD_PAD - MLA_QK))).reshape(n_l, MLA_Q_RANK, MLA_QK_PAD)
    w_ukv4 = w_ukv.reshape(n_l, MLA_KV_RANK, MLA_HEADS, MLA_NOPE + MLA_V)
    w_uk_p = jnp.pad(w_ukv4[..., :MLA_NOPE], ((0, 0), (0, 0), (0, 0), (0, MLA_HEAD_PAD - MLA_NOPE)))
    w_kv = jnp.concatenate([w_uk_p.reshape(n_l, MLA_KV_RANK, MLA_QK_PAD),
                            w_ukv4[..., MLA_NOPE:].reshape(n_l, MLA_KV_RANK, MLA_WIDTH)], axis=-1)
    vec = lambda a: a.reshape(n_l, 1, a.shape[-1])
    return {
        "w_in": w_in_p, "w_uq": w_uq_p.astype(BF16), "w_kv": w_kv.astype(BF16),
        "w_out": w_out.astype(BF16), "w_ff1": w_ff1.astype(BF16), "w_ff3": w_ff3.astype(BF16),
        "w_ff2": w_ff2.astype(BF16), "conv_w": conv_w,
        "q_norm_w": vec(q_norm_w), "kv_norm_w": vec(kv_norm_w),
        "ln1_g": vec(ln1_g), "ln1_b": vec(ln1_b), "ln2_g": vec(ln2_g), "ln2_b": vec(ln2_b),
    }


def kernel(x_prompt, x_sample, cache_diff_k, cache_diff_v, cache_mla_ckv, cache_mla_kpe, c, c_ctx,
           w_ada, b_ada, w_in, conv_w, lam_q1, lam_k1, lam_q2, lam_k2, diff_norm_w, q_norm_w, w_uq,
           kv_norm_w, w_ukv, w_out, ln1_g, ln1_b, w_ff1, w_ff3, w_ff2, ln2_g, ln2_b):
    n_l = w_in.shape[0]
    pb, ps, _ = x_prompt.shape
    sb, ss, _ = x_sample.shape
    past = cache_mla_ckv.shape[2]

    wts = _prep_weights(w_in, w_uq, w_ukv, w_out, w_ff1, w_ff3, w_ff2, conv_w, q_norm_w, kv_norm_w,
                        ln1_g, ln1_b, ln2_g, ln2_b)
    lamv = jnp.stack([lam_q1, lam_k1, lam_q2, lam_k2], axis=1)
    dnw = jnp.tile(diff_norm_w, (1, DIFF_HEADS)).reshape(n_l, 1, DIFF_WIDTH)

    cond = jnp.concatenate([c_ctx[None, :], c, jnp.zeros((COND_ROWS - 1 - sb, D_MODEL), F32)], axis=0)
    mod3 = _ada_call(cond, w_ada, b_ada).reshape(n_l * COND_ROWS * N_MOD, 1, D_MODEL)

    ck = jnp.transpose(cache_diff_k, (0, 1, 4, 2, 3, 5)).reshape(sb, n_l, past, DIFF_QK_COLS).astype(BF16)
    cv_t = jnp.swapaxes(cache_diff_v, -1, -2).reshape(sb, n_l, DIFF_WIDTH, past).astype(BF16)
    kpe_slab = jnp.pad(cache_mla_kpe, ((0, 0), (0, 0), (0, 0), (0, LANES - MLA_ROPE)))
    ckc, cvm_t = _cache_kv_call(cache_mla_ckv, kpe_slab, wts["w_kv"])

    rope_tabs = _rope_tables(ss)
    tm_in_p, tm_in, tm_tail, tq_s = 512, 1024, 512, 256

    def sample_row(tm):
        return lambda i: 1 + i // (ss // tm)

    xp = x_prompt.reshape(pb * ps, D_MODEL)
    xs = x_sample.reshape(sb * ss, D_MODEL)
    states = ()
    for l in range(n_l):
        lam_init = 0.8 - 0.6 * math.exp(-0.3 * l)

        row_p = lambda i: 0
        outs = _inproj_call(xp, mod3, wts, l, n_batch=pb, seq=ps, tm=tm_in_p, mod_row=row_p, rope_tabs=None,
                            n_layers=n_l, states=states)
        u, ab, qdt, kd, vdt, qmt, kc, vmt = outs[:8]
        states = tuple(outs[8:])
        yb, yc = _attn_call(lamv, dnw, qdt, kd, vdt, qmt, kc, vmt, None, l,
                            n_batch=pb, seq=ps, tq=ps, lam_init=lam_init)
        xp = _tail_call(xp, u, ab, yb, yc, mod3, wts, l, seq=ps, tm=tm_tail, mod_row=row_p)

        u, ab, qdt, kd, vdt, qmt, kc, vmt = _inproj_call(
            xs, mod3, wts, l, n_batch=sb, seq=ss, tm=tm_in, mod_row=sample_row(tm_in), rope_tabs=rope_tabs,
            n_layers=n_l, states=None)
        yb, yc = _attn_call(lamv, dnw, qdt, kd, vdt, qmt, kc, vmt, (ck, cv_t, ckc, cvm_t), l,
                            n_batch=sb, seq=ss, tq=tq_s, lam_init=lam_init)
        xs = _tail_call(xs, u, ab, yb, yc, mod3, wts, l, seq=ss, tm=tm_tail, mod_row=sample_row(tm_tail))

    return (xp.reshape(pb, ps, D_MODEL), xs.reshape(sb, ss, D_MODEL)) + states
```

```python
import functools
import math

import numpy as np
import jax
import jax.numpy as jnp
from jax import lax
from jax.experimental import pallas as pl
from jax.experimental.pallas import tpu as pltpu

F32 = jnp.float32
BF16 = jnp.bfloat16

D_MODEL = 1024
DEPTH = 2
GRID_W = 64
ROPE_BASE = 10000.0
CONV_W = 256
DIFF_HEADS = 4
DIFF_DK = 32
DIFF_DV = 64
DIFF_WIDTH = DIFF_HEADS * DIFF_DV
DIFF_QK_COLS = DIFF_HEADS * 2 * DIFF_DK
DIFF_SCALE = DIFF_DK ** -0.5
MLA_HEADS = 8
MLA_Q_RANK = 384
MLA_KV_RANK = 256
MLA_NOPE = 64
MLA_ROPE = 32
MLA_V = 64
MLA_QK = MLA_NOPE + MLA_ROPE
MLA_WIDTH = MLA_HEADS * MLA_V
MLA_SCALE = MLA_QK ** -0.5
IN_COLS = 3 * CONV_W + 2 * DIFF_QK_COLS + DIFF_WIDTH + MLA_Q_RANK + MLA_KV_RANK + MLA_ROPE
D_FF = 2816
DEEPNORM_ALPHA = (2 * DEPTH) ** 0.25
LOG2E = 1.4426950408889634

LANES = 128
SUBLANES = 8
BF16_ROWS = 16
MXU_DIM = 256
IN_COLS_PAD = 2304
MLA_HEAD_PAD = LANES
MLA_QK_PAD = MLA_HEADS * MLA_HEAD_PAD
N_MOD = 6
COND_ROWS = 8
FF_CHUNKS = ((0, 1536), (1536, 2816))
KEY_CHUNK = 64
LOOKAHEAD = 3
TAIL_SUB = 256
INPROJ_SUB = 256
VMEM_LIMIT = 56 * 2 ** 20

OFF_AX, OFF_AB, OFF_AC = 0, 256, 512
OFF_DQ, OFF_DK, OFF_DV = 768, 1024, 1280
OFF_CQ, OFF_CKV, OFF_KPE = 1536, 1920, 2176


def _cparams(n_grid):
    return pltpu.CompilerParams(dimension_semantics=("arbitrary",) * n_grid,
                                vmem_limit_bytes=VMEM_LIMIT)


def _layer_spec(tail, l):
    nd = len(tail)
    return pl.BlockSpec((None,) + tuple(tail), lambda *_: (l,) + (0,) * nd,
                        pipeline_mode=pl.Buffered(1))


def _sigmoid(x):
    return 1.0 / (1.0 + jnp.exp(-x))


def _layer_norm(x, g, b):
    mu = jnp.mean(x, axis=-1, keepdims=True)
    xc = x - mu
    var = jnp.mean(xc * xc, axis=-1, keepdims=True)
    return xc * lax.rsqrt(var + 1e-5) * g + b


def _rms_norm(x, w):
    ms = jnp.mean(x * x, axis=-1, keepdims=True)
    return x * lax.rsqrt(ms + 1e-6) * w


def _rope_slab(x, cos, sin_signed):
    lane = lax.broadcasted_iota(jnp.int32, x.shape, 1)
    lo = (lane % 16) < 8
    partner = jnp.where(lo, pltpu.roll(x, LANES - 8, axis=1), pltpu.roll(x, 8, axis=1))
    return x * cos + partner * sin_signed


def _rope_window(cos, sin_signed, lane0):
    lane = lax.broadcasted_iota(jnp.int32, cos.shape, 1)
    inside = (lane >= lane0) & (lane < lane0 + MLA_ROPE)
    return jnp.where(inside, cos, 1.0), jnp.where(inside, sin_signed, 0.0)


def _ada_kernel(cond_ref, w_ref, b_ref, o_ref):
    c = cond_ref[...]
    s = (c * _sigmoid(c)).astype(BF16)
    o_ref[0] = jnp.dot(s, w_ref[0].astype(BF16), preferred_element_type=F32) + b_ref[0]


def _ada_call(cond, w_ada, b_ada):
    n_l = w_ada.shape[0]
    tn = 2 * D_MODEL
    return pl.pallas_call(
        _ada_kernel,
        out_shape=jax.ShapeDtypeStruct((n_l, COND_ROWS, N_MOD * D_MODEL), F32),
        grid=(n_l, N_MOD * D_MODEL // tn),
        in_specs=[
            pl.BlockSpec((COND_ROWS, D_MODEL), lambda l, j: (0, 0)),
            pl.BlockSpec((1, D_MODEL, tn), lambda l, j: (l, 0, j)),
            pl.BlockSpec((1, 1, tn), lambda l, j: (l, 0, j)),
        ],
        out_specs=pl.BlockSpec((1, COND_ROWS, tn), lambda l, j: (l, 0, j)),
        compiler_params=_cparams(2),
        name="ada_mod",
    )(cond, w_ada, b_ada.reshape(n_l, 1, N_MOD * D_MODEL))


def _expand_kv(ckv, kpe_slab, w_kv_ref):
    kv = jnp.dot(ckv.astype(BF16), w_kv_ref[...], preferred_element_type=F32)
    kpe_shift = pltpu.roll(kpe_slab, MLA_NOPE, axis=1)
    k_cat = jnp.concatenate(
        [(kv[:, h * MLA_HEAD_PAD:(h + 1) * MLA_HEAD_PAD] + kpe_shift).astype(BF16) for h in range(MLA_HEADS)],
        axis=1)
    return k_cat, kv[:, MLA_QK_PAD:MLA_QK_PAD + MLA_WIDTH]


def _cache_kv_kernel(ckv_ref, kpe_ref, w_kv_ref, kc_ref, vmt_ref):
    k_cat, v_m = _expand_kv(ckv_ref[...], kpe_ref[...], w_kv_ref)
    kc_ref[...] = k_cat
    vmt_ref[...] = v_m.T.astype(BF16)


def _cache_kv_call(cache_ckv, cache_kpe_slab, w_kv):
    n_b, n_l, past, _ = cache_ckv.shape
    return pl.pallas_call(
        _cache_kv_kernel,
        out_shape=(jax.ShapeDtypeStruct((n_b, n_l, past, MLA_QK_PAD), BF16),
                   jax.ShapeDtypeStruct((n_b, n_l, MLA_WIDTH, past), BF16)),
        grid=(n_b, n_l),
        in_specs=[
            pl.BlockSpec((None, None, past, MLA_KV_RANK), lambda b, l: (b, l, 0, 0)),
            pl.BlockSpec((None, None, past, LANES), lambda b, l: (b, l, 0, 0)),
            pl.BlockSpec((None, MLA_KV_RANK, MLA_QK_PAD + MLA_WIDTH), lambda b, l: (l, 0, 0)),
        ],
        out_specs=(pl.BlockSpec((None, None, past, MLA_QK_PAD), lambda b, l: (b, l, 0, 0)),
                   pl.BlockSpec((None, None, MLA_WIDTH, past), lambda b, l: (b, l, 0, 0))),
        compiler_params=_cparams(2),
        name="cache_kv",
    )(cache_ckv, cache_kpe_slab, w_kv)


N_STATE = 4


def _inproj_kernel(*refs, rope, n_seq, seq, with_state, alias_state):
    it = iter(refs)
    x_ref, sc_ref, sh_ref = next(it), next(it), next(it)
    w_in_ref, qnw_ref, w_uq_ref, kvnw_ref, w_kv_ref = next(it), next(it), next(it), next(it), next(it)
    if rope:
        cd_ref, sd_ref = next(it), next(it)
    if alias_state:
        for _ in range(N_STATE):
            next(it)
    u_ref, ab_ref, qdt_ref, kd_ref, vdt_ref, qmt_ref, kc_ref, vmt_ref = (next(it) for _ in range(8))
    if with_state:
        stk_ref, stv_ref, stc_ref, stp_ref = (next(it) for _ in range(N_STATE))

    tm = x_ref.shape[0]
    sub = seq if n_seq else INPROJ_SUB
    st = [dict(r=r, rows=slice(r * sub, (r + 1) * sub)) for r in range(tm // sub)]

    def project(d):
        h = (x_ref[d["rows"], :] * (1.0 + sc_ref[0]) + sh_ref[0]).astype(BF16)
        d["proj"] = jnp.dot(h, w_in_ref[...], preferred_element_type=F32)

    def split(d):
        proj, rows = d.pop("proj"), d["rows"]
        u_ref[rows, :] = proj[:, OFF_AC:OFF_AC + CONV_W] * proj[:, OFF_AX:OFF_AX + CONV_W]
        ab_ref[rows, :] = proj[:, OFF_AB:OFF_AB + CONV_W]
        q = proj[:, OFF_DQ:OFF_DQ + DIFF_QK_COLS]
        k = proj[:, OFF_DK:OFF_DK + DIFF_QK_COLS]
        d["v"] = proj[:, OFF_DV:OFF_DV + DIFF_WIDTH]
        if rope:
            cd, sd = cd_ref[rows, :], sd_ref[rows, :]
            q = jnp.concatenate([_rope_slab(q[:, j * LANES:(j + 1) * LANES], cd, sd) for j in range(2)], axis=1)
            k = jnp.concatenate([_rope_slab(k[:, j * LANES:(j + 1) * LANES], cd, sd) for j in range(2)], axis=1)
        kd_ref[rows, :] = k.astype(BF16)
        d["q"], d["k"] = q, k
        d["cq"] = _rms_norm(proj[:, OFF_CQ:OFF_CQ + MLA_Q_RANK], qnw_ref[0]).astype(BF16)
        d["ckv"] = _rms_norm(proj[:, OFF_CKV:OFF_CKV + MLA_KV_RANK], kvnw_ref[0])
        kpe_slab = proj[:, OFF_KPE:OFF_KPE + LANES]
        if rope:
            kpe_slab = _rope_slab(kpe_slab, *_rope_window(cd_ref[rows, :], sd_ref[rows, :], 0))
        d["kpe"] = kpe_slab

    def up_project(d):
        d["qc"] = jnp.dot(d.pop("cq"), w_uq_ref[...], preferred_element_type=F32)
        d["k_cat"], d["v_m"] = _expand_kv(d["ckv"], d["kpe"], w_kv_ref)

    def emit(d):
        r, rows = d["r"], d["rows"]
        qc = d.pop("qc")
        if rope:
            cm, sm = _rope_window(cd_ref[rows, :], sd_ref[rows, :], MLA_NOPE)
            qc = jnp.concatenate(
                [_rope_slab(qc[:, hd * MLA_HEAD_PAD:(hd + 1) * MLA_HEAD_PAD], cm, sm) for hd in range(MLA_HEADS)],
                axis=1)
        kc_ref[rows, :] = d.pop("k_cat")
        q_d, q_m = d["q"] * (DIFF_SCALE * LOG2E), qc * (MLA_SCALE * LOG2E)
        for dst, val in ((qdt_ref, q_d), (vdt_ref, d["v"]), (qmt_ref, q_m), (vmt_ref, d.pop("v_m"))):
            val_t = val.T.astype(BF16)
            if n_seq:
                dst[r] = val_t
            else:
                dst[:, rows] = val_t
        if with_state:
            fills = [()] if alias_state else [(ll,) for ll in range(stc_ref.shape[1])]
            for ll in fills:
                stc_ref[(r,) + ll] = d["ckv"]
                stp_ref[(r,) + ll] = d["kpe"][:, :MLA_ROPE]
                for hm in range(2 * DIFF_HEADS):
                    stk_ref[(r,) + ll + (hm // 2, hm % 2)] = d["k"][:, hm * DIFF_DK:(hm + 1) * DIFF_DK]
                for hd in range(DIFF_HEADS):
                    stv_ref[(r,) + ll + (hd,)] = d["v"][:, hd * DIFF_DV:(hd + 1) * DIFF_DV]

    stages = [project, split, up_project, emit]
    for step in range(len(stages) + len(st) - 1):
        for r, d in enumerate(st):
            if 0 <= step - r < len(stages):
                stages[step - r](d)


def _inproj_call(x2d, mod3, wts, l, *, n_batch, seq, tm, mod_row, rope_tabs, n_layers, states):
    rows = n_batch * seq
    n_tiles = rows // tm
    rope = rope_tabs is not None
    with_state = states is not None
    alias_state = bool(states)
    n_seq = tm // seq
    t_per_seq = max(seq // tm, 1)

    def mod_spec(j):
        return pl.BlockSpec((1, 1, D_MODEL), lambda i: ((l * COND_ROWS + mod_row(i)) * N_MOD + j, 0, 0))

    def row_spec(width):
        return pl.BlockSpec((tm, width), lambda i: (i, 0))

    def feat_spec(width):
        if n_seq:
            return pl.BlockSpec((n_seq, width, seq), lambda i: (i, 0, 0))
        return pl.BlockSpec((None, width, tm), lambda i: (i // t_per_seq, 0, i % t_per_seq))

    in_specs = [
        row_spec(D_MODEL), mod_spec(1), mod_spec(0),
        _layer_spec((D_MODEL, IN_COLS_PAD), l),
        _layer_spec((1, MLA_Q_RANK), l),
        _layer_spec((MLA_Q_RANK, MLA_QK_PAD), l),
        _layer_spec((1, MLA_KV_RANK), l),
        _layer_spec((MLA_KV_RANK, MLA_QK_PAD + MLA_WIDTH), l),
    ]
    args = [x2d, mod3, mod3, wts["w_in"], wts["q_norm_w"], wts["w_uq"], wts["kv_norm_w"], wts["w_kv"]]
    if rope:
        in_specs += [pl.BlockSpec((tm, LANES), lambda i: (i % t_per_seq, 0))] * 2
        args += list(rope_tabs)

    out_shape = [
        jax.ShapeDtypeStruct((rows, CONV_W), F32),
        jax.ShapeDtypeStruct((rows, CONV_W), F32),
        jax.ShapeDtypeStruct((n_batch, DIFF_QK_COLS, seq), BF16),
        jax.ShapeDtypeStruct((rows, DIFF_QK_COLS), BF16),
        jax.ShapeDtypeStruct((n_batch, DIFF_WIDTH, seq), BF16),
        jax.ShapeDtypeStruct((n_batch, MLA_QK_PAD, seq), BF16),
        jax.ShapeDtypeStruct((rows, MLA_QK_PAD), BF16),
        jax.ShapeDtypeStruct((n_batch, MLA_WIDTH, seq), BF16),
    ]
    out_specs = [row_spec(CONV_W), row_spec(CONV_W), feat_spec(DIFF_QK_COLS), row_spec(DIFF_QK_COLS),
                 feat_spec(DIFF_WIDTH), feat_spec(MLA_QK_PAD), row_spec(MLA_QK_PAD), feat_spec(MLA_WIDTH)]
    aliases = {}
    if with_state:
        assert n_seq >= 1
        n_out = len(out_shape)
        out_shape += [
            jax.ShapeDtypeStruct((n_batch, n_layers, DIFF_HEADS, 2, seq, DIFF_DK), F32),
            jax.ShapeDtypeStruct((n_batch, n_layers, DIFF_HEADS, seq, DIFF_DV), F32),
            jax.ShapeDtypeStruct((n_batch, n_layers, seq, MLA_KV_RANK), F32),
            jax.ShapeDtypeStruct((n_batch, n_layers, seq, MLA_ROPE), F32),
        ]
        lyr, l_idx = (None, l) if alias_state else (n_layers, 0)
        out_specs += [
            pl.BlockSpec((n_seq, lyr, DIFF_HEADS, 2, seq, DIFF_DK), lambda i: (i, l_idx, 0, 0, 0, 0)),
            pl.BlockSpec((n_seq, lyr, DIFF_HEADS, seq, DIFF_DV), lambda i: (i, l_idx, 0, 0, 0)),
            pl.BlockSpec((n_seq, lyr, seq, MLA_KV_RANK), lambda i: (i, l_idx, 0, 0)),
            pl.BlockSpec((n_seq, lyr, seq, MLA_ROPE), lambda i: (i, l_idx, 0, 0)),
        ]
        if alias_state:
            for j, st in enumerate(states):
                aliases[len(args)] = n_out + j
                in_specs.append(pl.BlockSpec(memory_space=pl.ANY))
                args.append(st)

    kern = functools.partial(_inproj_kernel, rope=rope, n_seq=n_seq, seq=seq, with_state=with_state,
                             alias_state=alias_state)
    return pl.pallas_call(
        kern, out_shape=tuple(out_shape), grid=(n_tiles,),
        in_specs=in_specs, out_specs=tuple(out_specs), input_output_aliases=aliases,
        compiler_params=_cparams(1), name=f"inproj_l{l}_{'s' if rope else 'p'}",
    )(*args)


def _tree(op, xs):
    xs = list(xs)
    while len(xs) > 1:
        xs = [op(xs[i], xs[i + 1]) if i + 1 < len(xs) else xs[i] for i in range(0, len(xs), 2)]
    return xs[0]


def _slabs(x):
    return [x[j * SUBLANES:(j + 1) * SUBLANES] for j in range(x.shape[0] // SUBLANES)]


class _Head:
    def __init__(self, k_refs, rhs_fn, vt_refs):
        self.k_blocks = [(k, r0) for k in k_refs for r0 in range(0, k.shape[0], MXU_DIM)]
        self.v_blocks = [(v, c0) for v in vt_refs for c0 in range(0, v.shape[1], MXU_DIM)]
        self.rhs_fn = rhs_fn


def _run_heads(heads, finish, zero_ref, s_ring, p_ring):
    items = [(h, j) for h in range(len(heads)) for j in range(len(heads[h].k_blocks))]
    z_st = pl.multiple_of(zero_ref[0], MXU_DIM)
    z_ld = pl.multiple_of(zero_ref[1], MXU_DIM)
    rhs = {}
    blk_max = {}
    state = {}

    def issue_scores(i):
        h, j = items[i]
        if j == 0:
            rhs[h] = heads[h].rhs_fn()
        k_ref, r0 = heads[h].k_blocks[j]
        s = jnp.dot(k_ref[r0:r0 + MXU_DIM, :], rhs[h], preferred_element_type=F32)
        s_ring[i % len(s_ring)][pl.ds(z_st, MXU_DIM), :] = s
        blk_max[i] = _tree(jnp.maximum, _slabs(s))

    def consume(i):
        h, j = items[i]
        head = heads[h]
        s_buf, p_buf = s_ring[i % len(s_ring)], p_ring[i % len(p_ring)]
        m_blk = jnp.max(blk_max.pop(i), axis=0, keepdims=True)
        m_new = m_blk if j == 0 else jnp.maximum(state["m"], m_blk)
        for c0 in range(0, MXU_DIM, KEY_CHUNK):
            p = jnp.exp2(s_buf[pl.ds(z_ld + c0, KEY_CHUNK), :] - m_new)
            p_buf[pl.ds(z_st + c0, KEY_CHUNK), :] = p.astype(BF16)
        v_ref, c0 = head.v_blocks[j]
        v_ext = jnp.concatenate([v_ref[:, c0:c0 + MXU_DIM], jnp.ones((BF16_ROWS, MXU_DIM), BF16)], axis=0)
        pv = jnp.dot(v_ext, p_buf[pl.ds(z_ld, MXU_DIM), :], preferred_element_type=F32)
        if j == 0:
            state["acc"] = pv
        else:
            alpha = jnp.exp2(state["m"] - m_new)
            state["acc"] = state["acc"] * alpha + pv
        state["m"] = m_new
        if j == len(head.k_blocks) - 1:
            acc = state["acc"]
            dv = acc.shape[0] - BF16_ROWS
            finish(h, acc[:dv] * (1.0 / acc[dv:dv + 1]))

    for i in range(len(items) + LOOKAHEAD):
        if i < len(items):
            issue_scores(i)
        if i >= LOOKAHEAD:
            consume(i - LOOKAHEAD)


def _attn_kernel(*refs, lam_init, has_cache):
    it = iter(refs)
    zero_ref = next(it)
    lam_ref, dnw_ref = next(it), next(it)
    qdt_ref, kd_ref, vdt_ref, qmt_ref, kc_ref, vmt_ref = (next(it) for _ in range(6))
    kds, vdts, kcs, vmts = [kd_ref], [vdt_ref], [kc_ref], [vmt_ref]
    if has_cache:
        kds.append(next(it)); vdts.append(next(it)); kcs.append(next(it)); vmts.append(next(it))
    yb_ref, yc_ref = next(it), next(it)
    s_ring = [next(it) for _ in range(LOOKAHEAD + 1)]
    p_ring = [next(it) for _ in range(2)]

    lv = lam_ref[0]
    lam = (jnp.exp(jnp.sum(lv[0:1] * lv[1:2], axis=-1, keepdims=True))
           - jnp.exp(jnp.sum(lv[2:3] * lv[3:4], axis=-1, keepdims=True)) + lam_init)

    def diff_rhs(hm):
        def fn():
            q_dt = qdt_ref[...]
            feat = lax.broadcasted_iota(jnp.int32, q_dt.shape, 0)
            return jnp.where((feat >= hm * DIFF_DK) & (feat < (hm + 1) * DIFF_DK), q_dt, jnp.zeros_like(q_dt))
        return fn

    def mla_rhs(feats):
        return lambda: qmt_ref[feats, :]

    heads = []
    for hm in range(2 * DIFF_HEADS):
        rows = slice((hm // 2) * DIFF_DV, (hm // 2 + 1) * DIFF_DV)
        heads.append(_Head(kds, diff_rhs(hm), [vt.at[rows, :] for vt in vdts]))
    for hd in range(MLA_HEADS):
        feats = slice(hd * MLA_HEAD_PAD, (hd + 1) * MLA_HEAD_PAD)
        rows = slice(hd * MLA_V, (hd + 1) * MLA_V)
        heads.append(_Head([kc.at[:, feats] for kc in kcs], mla_rhs(feats), [vt.at[rows, :] for vt in vmts]))

    outs = []

    def finish(t, o):
        outs.append(o)
        if t == 2 * DIFF_HEADS - 1:
            y_heads = []
            for hd in range(DIFF_HEADS):
                d = outs[2 * hd] - lam * outs[2 * hd + 1]
                ms = jnp.mean(d * d, axis=0, keepdims=True)
                y_heads.append(d * lax.rsqrt(ms + 1e-6))
            y_b = jnp.concatenate(y_heads, axis=0).T
            yb_ref[...] = (y_b * dnw_ref[0] * (1.0 - lam_init)).astype(BF16)
        if t == len(heads) - 1:
            yc_ref[...] = jnp.concatenate(outs[2 * DIFF_HEADS:], axis=0).T.astype(BF16)

    _run_heads(heads, finish, zero_ref, s_ring, p_ring)


def _attn_call(lamv, dnw, qdt, kd, vdt, qmt, kc, vmt, cache, l, *, n_batch, seq, tq, lam_init):
    t_per_seq = seq // tq
    has_cache = cache is not None

    def q_spec(width):
        return pl.BlockSpec((None, width, tq), lambda b, t, z: (b, 0, t))

    def out_spec(width):
        return pl.BlockSpec((tq, width), lambda b, t, z: (b * t_per_seq + t, 0))

    in_specs = [
        pl.BlockSpec((1, 4, DIFF_DK), lambda b, t, z: (l, 0, 0)),
        pl.BlockSpec((1, 1, DIFF_WIDTH), lambda b, t, z: (l, 0, 0)),
        q_spec(DIFF_QK_COLS),
        pl.BlockSpec((seq, DIFF_QK_COLS), lambda b, t, z: (b, 0)),
        pl.BlockSpec((None, DIFF_WIDTH, seq), lambda b, t, z: (b, 0, 0)),
        q_spec(MLA_QK_PAD),
        pl.BlockSpec((seq, MLA_QK_PAD), lambda b, t, z: (b, 0)),
        pl.BlockSpec((None, MLA_WIDTH, seq), lambda b, t, z: (b, 0, 0)),
    ]
    args = [lamv, dnw, qdt, kd, vdt, qmt, kc, vmt]
    past = 0
    if has_cache:
        past = cache[0].shape[2]
        in_specs += [
            pl.BlockSpec((None, None, past, DIFF_QK_COLS), lambda b, t, z: (b, l, 0, 0)),
            pl.BlockSpec((None, None, DIFF_WIDTH, past), lambda b, t, z: (b, l, 0, 0)),
            pl.BlockSpec((None, None, past, MLA_QK_PAD), lambda b, t, z: (b, l, 0, 0)),
            pl.BlockSpec((None, None, MLA_WIDTH, past), lambda b, t, z: (b, l, 0, 0)),
        ]
        args += list(cache)
    rows = n_batch * seq
    ring = (2 * MXU_DIM, tq)
    return pl.pallas_call(
        functools.partial(_attn_kernel, lam_init=lam_init, has_cache=has_cache),
        out_shape=(jax.ShapeDtypeStruct((rows, DIFF_WIDTH), BF16),
                   jax.ShapeDtypeStruct((rows, MLA_WIDTH), BF16)),
        grid_spec=pltpu.PrefetchScalarGridSpec(
            num_scalar_prefetch=1,
            grid=(n_batch, t_per_seq),
            in_specs=in_specs,
            out_specs=(out_spec(DIFF_WIDTH), out_spec(MLA_WIDTH)),
            scratch_shapes=[pltpu.VMEM(ring, F32)] * (LOOKAHEAD + 1) + [pltpu.VMEM(ring, BF16)] * 2),
        compiler_params=_cparams(2), name=f"attn_l{l}_{'s' if has_cache else 'p'}",
    )(jnp.zeros((2,), jnp.int32), *args)


def _tail_kernel(x_ref, u_ref, up_ref, un_ref, ab_ref, yb_ref, yc_ref, cw_ref, g1_ref, sh2_ref, sc2_ref, g2_ref,
                 ln1g_ref, ln1b_ref, ln2g_ref, ln2b_ref, w_out_ref, w1_ref, w3_ref, w2_ref, o_ref, *, seq):
    tm = x_ref.shape[0]
    u = u_ref[...]
    row = lax.broadcasted_iota(jnp.int32, (tm, 1), 0)
    pos = (row + pl.program_id(0) * tm) % seq
    u_prev = jnp.where(row == 0, up_ref[SUBLANES - 1:SUBLANES, :], pltpu.roll(u, 1, axis=0))
    u_prev = jnp.where(pos == 0, 0.0, u_prev)
    u_next = jnp.where(row == tm - 1, un_ref[0:1, :], pltpu.roll(u, tm - 1, axis=0))
    u_next = jnp.where(pos == seq - 1, 0.0, u_next)
    cw = cw_ref[0]
    y_a = (ab_ref[...] * (u_prev * cw[0:1] + u * cw[1:2] + u_next * cw[2:3])).astype(BF16)

    n_sub = tm // TAIL_SUB
    st = [dict(rows=slice(r * TAIL_SUB, (r + 1) * TAIL_SUB)) for r in range(n_sub)]

    def out_proj(d):
        y_cat = jnp.concatenate([y_a[d["rows"]], yb_ref[d["rows"], :], yc_ref[d["rows"], :]], axis=1)
        d["y"] = jnp.dot(y_cat, w_out_ref[...], preferred_element_type=F32)

    def norm1(d):
        d["x1"] = _layer_norm(DEEPNORM_ALPHA * x_ref[d["rows"], :] + g1_ref[0] * d.pop("y"),
                              ln1g_ref[0], ln1b_ref[0])
        d["hf"] = (d["x1"] * (1.0 + sc2_ref[0]) + sh2_ref[0]).astype(BF16)
        d["f"] = None

    def ffn_chunk(c0, c1):
        def stage(d):
            g = jnp.dot(d["hf"], w1_ref[:, c0:c1], preferred_element_type=F32)
            a = (g * _sigmoid(g) * jnp.dot(d["hf"], w3_ref[:, c0:c1], preferred_element_type=F32)).astype(BF16)
            t = jnp.dot(a, w2_ref[c0:c1, :], preferred_element_type=F32)
            d["f"] = t if d["f"] is None else d["f"] + t
        return stage

    def norm2(d):
        o_ref[d["rows"], :] = _layer_norm(DEEPNORM_ALPHA * d.pop("x1") + g2_ref[0] * d.pop("f"),
                                          ln2g_ref[0], ln2b_ref[0])

    stages = [out_proj, norm1] + [ffn_chunk(c0, c1) for c0, c1 in FF_CHUNKS] + [norm2]
    for step in range(len(stages) + n_sub - 1):
        for r in range(n_sub):
            if 0 <= step - r < len(stages):
                stages[step - r](st[r])


def _tail_call(x2d, u, ab, yb, yc, mod3, wts, l, *, seq, tm, mod_row):
    rows = x2d.shape[0]
    n_tiles = rows // tm
    n_halo = rows // SUBLANES

    def mod_spec(j):
        return pl.BlockSpec((1, 1, D_MODEL), lambda i: ((l * COND_ROWS + mod_row(i)) * N_MOD + j, 0, 0))

    def row_spec(width):
        return pl.BlockSpec((tm, width), lambda i: (i, 0))

    in_specs = [
        row_spec(D_MODEL), row_spec(CONV_W),
        pl.BlockSpec((SUBLANES, CONV_W), lambda i: (jnp.maximum(i * (tm // SUBLANES) - 1, 0), 0)),
        pl.BlockSpec((SUBLANES, CONV_W), lambda i: (jnp.minimum((i + 1) * (tm // SUBLANES), n_halo - 1), 0)),
        row_spec(CONV_W), row_spec(DIFF_WIDTH), row_spec(MLA_WIDTH),
        pl.BlockSpec((1, 3, CONV_W), lambda i: (l, 0, 0)),
        mod_spec(2), mod_spec(3), mod_spec(4), mod_spec(5),
        _layer_spec((1, D_MODEL), l), _layer_spec((1, D_MODEL), l),
        _layer_spec((1, D_MODEL), l), _layer_spec((1, D_MODEL), l),
        _layer_spec((D_MODEL, D_MODEL), l),
        _layer_spec((D_MODEL, D_FF), l), _layer_spec((D_MODEL, D_FF), l), _layer_spec((D_FF, D_MODEL), l),
    ]
    return pl.pallas_call(
        functools.partial(_tail_kernel, seq=seq),
        out_shape=jax.ShapeDtypeStruct((rows, D_MODEL), F32),
        grid=(n_tiles,),
        in_specs=in_specs,
        out_specs=row_spec(D_MODEL),
        compiler_params=_cparams(1), name=f"tail_l{l}",
    )(x2d, u, u, u, ab, yb, yc, wts["conv_w"], mod3, mod3, mod3, mod3,
      wts["ln1_g"], wts["ln1_b"], wts["ln2_g"], wts["ln2_b"],
      wts["w_out"], wts["w_ff1"], wts["w_ff3"], wts["w_ff2"])


def _rope_tables(n_tokens):
    rows = n_tokens // GRID_W
    row = np.repeat(np.arange(rows), GRID_W).astype(np.float32)
    col = np.tile(np.arange(GRID_W), rows).astype(np.float32)
    half = MLA_ROPE // 4
    freqs = (np.float32(ROPE_BASE) ** (-np.arange(half, dtype=np.float32) / np.float32(half))).astype(np.float32)
    ang_r = row[:, None] * freqs[None, :]
    ang_c = col[:, None] * freqs[None, :]
    cos32 = np.concatenate([np.cos(ang_r), np.cos(ang_r), np.cos(ang_c), np.cos(ang_c)], axis=1)
    sin32 = np.concatenate([-np.sin(ang_r), np.sin(ang_r), -np.sin(ang_c), np.sin(ang_c)], axis=1)
    reps = LANES // cos32.shape[1]
    return tuple(jnp.asarray(np.tile(t, (1, reps)).astype(np.float32)) for t in (cos32, sin32))


def _prep_weights(w_in, w_uq, w_ukv, w_out, w_ff1, w_ff3, w_ff2, conv_w, q_norm_w, kv_norm_w,
                  ln1_g, ln1_b, ln2_g, ln2_b):
    n_l = w_in.shape[0]
    w_in_p = jnp.concatenate(
        [w_in.astype(BF16), jnp.zeros((n_l, D_MODEL, IN_COLS_PAD - IN_COLS), BF16)], axis=-1)
    w_uq_p = jnp.pad(w_uq.reshape(n_l, MLA_Q_RANK, MLA_HEADS, MLA_QK),
                     ((0, 0), (0, 0), (0, 0), (0, MLA_HEAD_PAD - MLA_QK))).reshape(n_l, MLA_Q_RANK, MLA_QK_PAD)
    w_ukv4 = w_ukv.reshape(n_l, MLA_KV_RANK, MLA_HEADS, MLA_NOPE + MLA_V)
    w_uk_p = jnp.pad(w_ukv4[..., :MLA_NOPE], ((0, 0), (0, 0), (0, 0), (0, MLA_HEAD_PAD - MLA_NOPE)))
    w_kv = jnp.concatenate([w_uk_p.reshape(n_l, MLA_KV_RANK, MLA_QK_PAD),
                            w_ukv4[..., MLA_NOPE:].reshape(n_l, MLA_KV_RANK, MLA_WIDTH)], axis=-1)
    vec = lambda a: a.reshape(n_l, 1, a.shape[-1])
    return {
        "w_in": w_in_p, "w_uq": w_uq_p.astype(BF16), "w_kv": w_kv.astype(BF16),
        "w_out": w_out.astype(BF16), "w_ff1": w_ff1.astype(BF16), "w_ff3": w_ff3.astype(BF16),
        "w_ff2": w_ff2.astype(BF16), "conv_w": conv_w,
        "q_norm_w": vec(q_norm_w), "kv_norm_w": vec(kv_norm_w),
        "ln1_g": vec(ln1_g), "ln1_b": vec(ln1_b), "ln2_g": vec(ln2_g), "ln2_b": vec(ln2_b),
    }


def kernel(x_prompt, x_sample, cache_diff_k, cache_diff_v, cache_mla_ckv, cache_mla_kpe, c, c_ctx,
           w_ada, b_ada, w_in, conv_w, lam_q1, lam_k1, lam_q2, lam_k2, diff_norm_w, q_norm_w, w_uq,
           kv_norm_w, w_ukv, w_out, ln1_g, ln1_b, w_ff1, w_ff3, w_ff2, ln2_g, ln2_b):
    n_l = w_in.shape[0]
    pb, ps, _ = x_prompt.shape
    sb, ss, _ = x_sample.shape
    past = cache_mla_ckv.shape[2]

    wts = _prep_weights(w_in, w_uq, w_ukv, w_out, w_ff1, w_ff3, w_ff2, conv_w, q_norm_w, kv_norm_w,
                        ln1_g, ln1_b, ln2_g, ln2_b)
    lamv = jnp.stack([lam_q1, lam_k1, lam_q2, lam_k2], axis=1)
    dnw = jnp.tile(diff_norm_w, (1, DIFF_HEADS)).reshape(n_l, 1, DIFF_WIDTH)

    cond = jnp.concatenate([c_ctx[None, :], c, jnp.zeros((COND_ROWS - 1 - sb, D_MODEL), F32)], axis=0)
    mod3 = _ada_call(cond, w_ada, b_ada).reshape(n_l * COND_ROWS * N_MOD, 1, D_MODEL)

    ck = jnp.transpose(cache_diff_k, (0, 1, 4, 2, 3, 5)).reshape(sb, n_l, past, DIFF_QK_COLS).astype(BF16)
    cv_t = jnp.swapaxes(cache_diff_v, -1, -2).reshape(sb, n_l, DIFF_WIDTH, past).astype(BF16)
    kpe_slab = jnp.pad(cache_mla_kpe, ((0, 0), (0, 0), (0, 0), (0, LANES - MLA_ROPE)))
    ckc, cvm_t = _cache_kv_call(cache_mla_ckv, kpe_slab, wts["w_kv"])

    rope_tabs = _rope_tables(ss)
    tm_in_p, tm_in, tm_tail, tq_s = 512, 1024, 512, 512

    def sample_row(tm):
        return lambda i: 1 + i // (ss // tm)

    xp = x_prompt.reshape(pb * ps, D_MODEL)
    xs = x_sample.reshape(sb * ss, D_MODEL)
    states = ()
    for l in range(n_l):
        lam_init = 0.8 - 0.6 * math.exp(-0.3 * l)

        row_p = lambda i: 0
        outs = _inproj_call(xp, mod3, wts, l, n_batch=pb, seq=ps, tm=tm_in_p, mod_row=row_p, rope_tabs=None,
                            n_layers=n_l, states=states)
        u, ab, qdt, kd, vdt, qmt, kc, vmt = outs[:8]
        states = tuple(outs[8:])
        yb, yc = _attn_call(lamv, dnw, qdt, kd, vdt, qmt, kc, vmt, None, l,
                            n_batch=pb, seq=ps, tq=ps, lam_init=lam_init)
        xp = _tail_call(xp, u, ab, yb, yc, mod3, wts, l, seq=ps, tm=tm_tail, mod_row=row_p)

        u, ab, qdt, kd, vdt, qmt, kc, vmt = _inproj_call(
            xs, mod3, wts, l, n_batch=sb, seq=ss, tm=tm_in, mod_row=sample_row(tm_in), rope_tabs=rope_tabs,
            n_layers=n_l, states=None)
        yb, yc = _attn_call(lamv, dnw, qdt, kd, vdt, qmt, kc, vmt, (ck, cv_t, ckc, cvm_t), l,
                            n_batch=sb, seq=ss, tq=tq_s, lam_init=lam_init)
        xs = _tail_call(xs, u, ab, yb, yc, mod3, wts, l, seq=ss, tm=tm_tail, mod_row=sample_row(tm_tail))

    return (xp.reshape(pb, ps, D_MODEL), xs.reshape(sb, ss, D_MODEL)) + states
```

```python
import functools
import math

import numpy as np
import jax
import jax.numpy as jnp
from jax import lax
from jax.experimental import pallas as pl
from jax.experimental.pallas import tpu as pltpu

F32 = jnp.float32
BF16 = jnp.bfloat16

D_MODEL = 1024
DEPTH = 2
GRID_W = 64
ROPE_BASE = 10000.0
CONV_W = 256
DIFF_HEADS = 4
DIFF_DK = 32
DIFF_DV = 64
DIFF_WIDTH = DIFF_HEADS * DIFF_DV
DIFF_QK_COLS = DIFF_HEADS * 2 * DIFF_DK
DIFF_SCALE = DIFF_DK ** -0.5
MLA_HEADS = 8
MLA_Q_RANK = 384
MLA_KV_RANK = 256
MLA_NOPE = 64
MLA_ROPE = 32
MLA_V = 64
MLA_QK = MLA_NOPE + MLA_ROPE
MLA_WIDTH = MLA_HEADS * MLA_V
MLA_SCALE = MLA_QK ** -0.5
IN_COLS = 3 * CONV_W + 2 * DIFF_QK_COLS + DIFF_WIDTH + MLA_Q_RANK + MLA_KV_RANK + MLA_ROPE
D_FF = 2816
DEEPNORM_ALPHA = (2 * DEPTH) ** 0.25
LOG2E = 1.4426950408889634

LANES = 128
SUBLANES = 8
BF16_ROWS = 16
MXU_DIM = 256
IN_COLS_PAD = 2304
MLA_HEAD_PAD = LANES
MLA_QK_PAD = MLA_HEADS * MLA_HEAD_PAD
N_MOD = 6
COND_ROWS = 8
FF_CHUNKS = ((0, 1536), (1536, 2816))
KEY_CHUNK = 64
LOOKAHEAD = 3
TAIL_SUB = 256
INPROJ_SUB = 256
VMEM_LIMIT = 56 * 2 ** 20

OFF_AX, OFF_AB, OFF_AC = 0, 256, 512
OFF_DQ, OFF_DK, OFF_DV = 768, 1024, 1280
OFF_CQ, OFF_CKV, OFF_KPE = 1536, 1920, 2176


def _cparams(n_grid):
    return pltpu.CompilerParams(dimension_semantics=("arbitrary",) * n_grid,
                                vmem_limit_bytes=VMEM_LIMIT)


def _layer_spec(tail, l):
    nd = len(tail)
    return pl.BlockSpec((None,) + tuple(tail), lambda *_: (l,) + (0,) * nd,
                        pipeline_mode=pl.Buffered(1))


def _sigmoid(x):
    return 1.0 / (1.0 + jnp.exp(-x))


def _layer_norm(x, g, b):
    mu = jnp.mean(x, axis=-1, keepdims=True)
    xc = x - mu
    var = jnp.mean(xc * xc, axis=-1, keepdims=True)
    return xc * lax.rsqrt(var + 1e-5) * g + b


def _rms_norm(x, w):
    ms = jnp.mean(x * x, axis=-1, keepdims=True)
    return x * lax.rsqrt(ms + 1e-6) * w


def _rope_slab(x, cos, sin_signed):
    lane = lax.broadcasted_iota(jnp.int32, x.shape, 1)
    lo = (lane % 16) < 8
    partner = jnp.where(lo, pltpu.roll(x, LANES - 8, axis=1), pltpu.roll(x, 8, axis=1))
    return x * cos + partner * sin_signed


def _rope_window(cos, sin_signed, lane0):
    lane = lax.broadcasted_iota(jnp.int32, cos.shape, 1)
    inside = (lane >= lane0) & (lane < lane0 + MLA_ROPE)
    return jnp.where(inside, cos, 1.0), jnp.where(inside, sin_signed, 0.0)


def _ada_kernel(cond_ref, w_ref, b_ref, o_ref):
    c = cond_ref[...]
    s = (c * _sigmoid(c)).astype(BF16)
    o_ref[0] = jnp.dot(s, w_ref[0].astype(BF16), preferred_element_type=F32) + b_ref[0]


def _ada_call(cond, w_ada, b_ada):
    n_l = w_ada.shape[0]
    tn = 2 * D_MODEL
    return pl.pallas_call(
        _ada_kernel,
        out_shape=jax.ShapeDtypeStruct((n_l, COND_ROWS, N_MOD * D_MODEL), F32),
        grid=(n_l, N_MOD * D_MODEL // tn),
        in_specs=[
            pl.BlockSpec((COND_ROWS, D_MODEL), lambda l, j: (0, 0)),
            pl.BlockSpec((1, D_MODEL, tn), lambda l, j: (l, 0, j)),
            pl.BlockSpec((1, 1, tn), lambda l, j: (l, 0, j)),
        ],
        out_specs=pl.BlockSpec((1, COND_ROWS, tn), lambda l, j: (l, 0, j)),
        compiler_params=_cparams(2),
        name="ada_mod",
    )(cond, w_ada, b_ada.reshape(n_l, 1, N_MOD * D_MODEL))


def _expand_kv(ckv, kpe_slab, w_kv_ref):
    kv = jnp.dot(ckv.astype(BF16), w_kv_ref[...], preferred_element_type=F32)
    kpe_shift = pltpu.roll(kpe_slab, MLA_NOPE, axis=1)
    k_cat = jnp.concatenate(
        [(kv[:, h * MLA_HEAD_PAD:(h + 1) * MLA_HEAD_PAD] + kpe_shift).astype(BF16) for h in range(MLA_HEADS)],
        axis=1)
    return k_cat, kv[:, MLA_QK_PAD:MLA_QK_PAD + MLA_WIDTH]


def _cache_kv_kernel(ckv_ref, kpe_ref, w_kv_ref, kc_ref, vmt_ref):
    k_cat, v_m = _expand_kv(ckv_ref[...], kpe_ref[...], w_kv_ref)
    kc_ref[...] = k_cat
    vmt_ref[...] = v_m.T.astype(BF16)


def _cache_kv_call(cache_ckv, cache_kpe_slab, w_kv):
    n_b, n_l, past, _ = cache_ckv.shape
    return pl.pallas_call(
        _cache_kv_kernel,
        out_shape=(jax.ShapeDtypeStruct((n_b, n_l, past, MLA_QK_PAD), BF16),
                   jax.ShapeDtypeStruct((n_b, n_l, MLA_WIDTH, past), BF16)),
        grid=(n_b, n_l),
        in_specs=[
            pl.BlockSpec((None, None, past, MLA_KV_RANK), lambda b, l: (b, l, 0, 0)),
            pl.BlockSpec((None, None, past, LANES), lambda b, l: (b, l, 0, 0)),
            pl.BlockSpec((None, MLA_KV_RANK, MLA_QK_PAD + MLA_WIDTH), lambda b, l: (l, 0, 0)),
        ],
        out_specs=(pl.BlockSpec((None, None, past, MLA_QK_PAD), lambda b, l: (b, l, 0, 0)),
                   pl.BlockSpec((None, None, MLA_WIDTH, past), lambda b, l: (b, l, 0, 0))),
        compiler_params=_cparams(2),
        name="cache_kv",
    )(cache_ckv, cache_kpe_slab, w_kv)


N_STATE = 4


def _inproj_kernel(*refs, rope, n_seq, seq, with_state, alias_state):
    it = iter(refs)
    x_ref, sc_ref, sh_ref = next(it), next(it), next(it)
    w_in_ref, qnw_ref, w_uq_ref, kvnw_ref, w_kv_ref = next(it), next(it), next(it), next(it), next(it)
    if rope:
        cd_ref, sd_ref = next(it), next(it)
    if alias_state:
        for _ in range(N_STATE):
            next(it)
    u_ref, ab_ref, qdt_ref, kd_ref, vdt_ref, qmt_ref, kc_ref, vmt_ref = (next(it) for _ in range(8))
    if with_state:
        stk_ref, stv_ref, stc_ref, stp_ref = (next(it) for _ in range(N_STATE))

    tm = x_ref.shape[0]
    sub = seq if n_seq else INPROJ_SUB
    st = [dict(r=r, rows=slice(r * sub, (r + 1) * sub)) for r in range(tm // sub)]

    def project(d):
        h = (x_ref[d["rows"], :] * (1.0 + sc_ref[0]) + sh_ref[0]).astype(BF16)
        d["proj"] = jnp.dot(h, w_in_ref[...], preferred_element_type=F32)

    def split(d):
        proj, rows = d.pop("proj"), d["rows"]
        u_ref[rows, :] = proj[:, OFF_AC:OFF_AC + CONV_W] * proj[:, OFF_AX:OFF_AX + CONV_W]
        ab_ref[rows, :] = proj[:, OFF_AB:OFF_AB + CONV_W]
        q = proj[:, OFF_DQ:OFF_DQ + DIFF_QK_COLS]
        k = proj[:, OFF_DK:OFF_DK + DIFF_QK_COLS]
        d["v"] = proj[:, OFF_DV:OFF_DV + DIFF_WIDTH]
        if rope:
            cd, sd = cd_ref[rows, :], sd_ref[rows, :]
            q = jnp.concatenate([_rope_slab(q[:, j * LANES:(j + 1) * LANES], cd, sd) for j in range(2)], axis=1)
            k = jnp.concatenate([_rope_slab(k[:, j * LANES:(j + 1) * LANES], cd, sd) for j in range(2)], axis=1)
        kd_ref[rows, :] = k.astype(BF16)
        d["q"], d["k"] = q, k
        d["cq"] = _rms_norm(proj[:, OFF_CQ:OFF_CQ + MLA_Q_RANK], qnw_ref[0]).astype(BF16)
        d["ckv"] = _rms_norm(proj[:, OFF_CKV:OFF_CKV + MLA_KV_RANK], kvnw_ref[0])
        kpe_slab = proj[:, OFF_KPE:OFF_KPE + LANES]
        if rope:
            kpe_slab = _rope_slab(kpe_slab, *_rope_window(cd_ref[rows, :], sd_ref[rows, :], 0))
        d["kpe"] = kpe_slab

    def up_project(d):
        d["qc"] = jnp.dot(d.pop("cq"), w_uq_ref[...], preferred_element_type=F32)
        d["k_cat"], d["v_m"] = _expand_kv(d["ckv"], d["kpe"], w_kv_ref)

    def emit(d):
        r, rows = d["r"], d["rows"]
        qc = d.pop("qc")
        if rope:
            cm, sm = _rope_window(cd_ref[rows, :], sd_ref[rows, :], MLA_NOPE)
            qc = jnp.concatenate(
                [_rope_slab(qc[:, hd * MLA_HEAD_PAD:(hd + 1) * MLA_HEAD_PAD], cm, sm) for hd in range(MLA_HEADS)],
                axis=1)
        kc_ref[rows, :] = d.pop("k_cat")
        q_d, q_m = d["q"] * (DIFF_SCALE * LOG2E), qc * (MLA_SCALE * LOG2E)
        for dst, val in ((qdt_ref, q_d), (vdt_ref, d["v"]), (qmt_ref, q_m), (vmt_ref, d.pop("v_m"))):
            val_t = val.T.astype(BF16)
            if n_seq:
                dst[r] = val_t
            else:
                dst[:, rows] = val_t
        if with_state:
            fills = [()] if alias_state else [(ll,) for ll in range(stc_ref.shape[1])]
            for ll in fills:
                stc_ref[(r,) + ll] = d["ckv"]
                stp_ref[(r,) + ll] = d["kpe"][:, :MLA_ROPE]
                for hm in range(2 * DIFF_HEADS):
                    stk_ref[(r,) + ll + (hm // 2, hm % 2)] = d["k"][:, hm * DIFF_DK:(hm + 1) * DIFF_DK]
                for hd in range(DIFF_HEADS):
                    stv_ref[(r,) + ll + (hd,)] = d["v"][:, hd * DIFF_DV:(hd + 1) * DIFF_DV]

    stages = [project, split, up_project, emit]
    for step in range(len(stages) + len(st) - 1):
        for r, d in enumerate(st):
            if 0 <= step - r < len(stages):
                stages[step - r](d)


def _inproj_call(x2d, mod3, wts, l, *, n_batch, seq, tm, mod_row, rope_tabs, n_layers, states):
    rows = n_batch * seq
    n_tiles = rows // tm
    rope = rope_tabs is not None
    with_state = states is not None
    alias_state = bool(states)
    n_seq = tm // seq
    t_per_seq = max(seq // tm, 1)

    def mod_spec(j):
        return pl.BlockSpec((1, 1, D_MODEL), lambda i: ((l * COND_ROWS + mod_row(i)) * N_MOD + j, 0, 0))

    def row_spec(width):
        return pl.BlockSpec((tm, width), lambda i: (i, 0))

    def feat_spec(width):
        if n_seq:
            return pl.BlockSpec((n_seq, width, seq), lambda i: (i, 0, 0))
        return pl.BlockSpec((None, width, tm), lambda i: (i // t_per_seq, 0, i % t_per_seq))

    in_specs = [
        row_spec(D_MODEL), mod_spec(1), mod_spec(0),
        _layer_spec((D_MODEL, IN_COLS_PAD), l),
        _layer_spec((1, MLA_Q_RANK), l),
        _layer_spec((MLA_Q_RANK, MLA_QK_PAD), l),
        _layer_spec((1, MLA_KV_RANK), l),
        _layer_spec((MLA_KV_RANK, MLA_QK_PAD + MLA_WIDTH), l),
    ]
    args = [x2d, mod3, mod3, wts["w_in"], wts["q_norm_w"], wts["w_uq"], wts["kv_norm_w"], wts["w_kv"]]
    if rope:
        in_specs += [pl.BlockSpec((tm, LANES), lambda i: (i % t_per_seq, 0))] * 2
        args += list(rope_tabs)

    out_shape = [
        jax.ShapeDtypeStruct((rows, CONV_W), F32),
        jax.ShapeDtypeStruct((rows, CONV_W), F32),
        jax.ShapeDtypeStruct((n_batch, DIFF_QK_COLS, seq), BF16),
        jax.ShapeDtypeStruct((rows, DIFF_QK_COLS), BF16),
        jax.ShapeDtypeStruct((n_batch, DIFF_WIDTH, seq), BF16),
        jax.ShapeDtypeStruct((n_batch, MLA_QK_PAD, seq), BF16),
        jax.ShapeDtypeStruct((rows, MLA_QK_PAD), BF16),
        jax.ShapeDtypeStruct((n_batch, MLA_WIDTH, seq), BF16),
    ]
    out_specs = [row_spec(CONV_W), row_spec(CONV_W), feat_spec(DIFF_QK_COLS), row_spec(DIFF_QK_COLS),
                 feat_spec(DIFF_WIDTH), feat_spec(MLA_QK_PAD), row_spec(MLA_QK_PAD), feat_spec(MLA_WIDTH)]
    aliases = {}
    if with_state:
        assert n_seq >= 1
        n_out = len(out_shape)
        out_shape += [
            jax.ShapeDtypeStruct((n_batch, n_layers, DIFF_HEADS, 2, seq, DIFF_DK), F32),
            jax.ShapeDtypeStruct((n_batch, n_layers, DIFF_HEADS, seq, DIFF_DV), F32),
            jax.ShapeDtypeStruct((n_batch, n_layers, seq, MLA_KV_RANK), F32),
            jax.ShapeDtypeStruct((n_batch, n_layers, seq, MLA_ROPE), F32),
        ]
        lyr, l_idx = (None, l) if alias_state else (n_layers, 0)
        out_specs += [
            pl.BlockSpec((n_seq, lyr, DIFF_HEADS, 2, seq, DIFF_DK), lambda i: (i, l_idx, 0, 0, 0, 0)),
            pl.BlockSpec((n_seq, lyr, DIFF_HEADS, seq, DIFF_DV), lambda i: (i, l_idx, 0, 0, 0)),
            pl.BlockSpec((n_seq, lyr, seq, MLA_KV_RANK), lambda i: (i, l_idx, 0, 0)),
            pl.BlockSpec((n_seq, lyr, seq, MLA_ROPE), lambda i: (i, l_idx, 0, 0)),
        ]
        if alias_state:
            for j, st in enumerate(states):
                aliases[len(args)] = n_out + j
                in_specs.append(pl.BlockSpec(memory_space=pl.ANY))
                args.append(st)

    kern = functools.partial(_inproj_kernel, rope=rope, n_seq=n_seq, seq=seq, with_state=with_state,
                             alias_state=alias_state)
    return pl.pallas_call(
        kern, out_shape=tuple(out_shape), grid=(n_tiles,),
        in_specs=in_specs, out_specs=tuple(out_specs), input_output_aliases=aliases,
        compiler_params=_cparams(1), name=f"inproj_l{l}_{'s' if rope else 'p'}",
    )(*args)


def _tree(op, xs):
    xs = list(xs)
    while len(xs) > 1:
        xs = [op(xs[i], xs[i + 1]) if i + 1 < len(xs) else xs[i] for i in range(0, len(xs), 2)]
    return xs[0]


def _slabs(x):
    return [x[j * SUBLANES:(j + 1) * SUBLANES] for j in range(x.shape[0] // SUBLANES)]


class _Head:
    def __init__(self, k_refs, rhs_fn, vt_refs):
        self.k_blocks = [(k, r0) for k in k_refs for r0 in range(0, k.shape[0], MXU_DIM)]
        self.v_blocks = [(v, c0) for v in vt_refs for c0 in range(0, v.shape[1], MXU_DIM)]
        self.rhs_fn = rhs_fn


def _run_heads(heads, finish, zero_ref, s_ring, p_ring):
    items = [(h, j) for h in range(len(heads)) for j in range(len(heads[h].k_blocks))]
    z_st = pl.multiple_of(zero_ref[0], MXU_DIM)
    z_ld = pl.multiple_of(zero_ref[1], MXU_DIM)
    rhs = {}
    blk_max = {}
    state = {}

    def issue_scores(i):
        h, j = items[i]
        if j == 0:
            rhs[h] = heads[h].rhs_fn()
        k_ref, r0 = heads[h].k_blocks[j]
        s = jnp.dot(k_ref[r0:r0 + MXU_DIM, :], rhs[h], preferred_element_type=F32)
        s_ring[i % len(s_ring)][pl.ds(z_st, MXU_DIM), :] = s
        blk_max[i] = _tree(jnp.maximum, _slabs(s))

    def consume(i):
        h, j = items[i]
        head = heads[h]
        s_buf, p_buf = s_ring[i % len(s_ring)], p_ring[i % len(p_ring)]
        blk = blk_max.pop(i)
        if j % 2 == 0:
            if j + 1 < len(head.k_blocks):
                blk = jnp.maximum(blk, blk_max[i + 1])
            m_blk = jnp.max(blk, axis=0, keepdims=True)
            m_new = m_blk if j == 0 else jnp.maximum(state["m"], m_blk)
        else:
            m_new = state["m"]
        for c0 in range(0, MXU_DIM, KEY_CHUNK):
            p = jnp.exp2(s_buf[pl.ds(z_ld + c0, KEY_CHUNK), :] - m_new)
            p_buf[pl.ds(z_st + c0, KEY_CHUNK), :] = p.astype(BF16)
        v_ref, c0 = head.v_blocks[j]
        v_ext = jnp.concatenate([v_ref[:, c0:c0 + MXU_DIM], jnp.ones((BF16_ROWS, MXU_DIM), BF16)], axis=0)
        pv = jnp.dot(v_ext, p_buf[pl.ds(z_ld, MXU_DIM), :], preferred_element_type=F32)
        if j == 0:
            state["acc"] = pv
        elif j % 2 == 0:
            alpha = jnp.exp2(state["m"] - m_new)
            state["acc"] = state["acc"] * alpha + pv
        else:
            state["acc"] = state["acc"] + pv
        state["m"] = m_new
        if j == len(head.k_blocks) - 1:
            acc = state["acc"]
            dv = acc.shape[0] - BF16_ROWS
            finish(h, acc[:dv] * (1.0 / acc[dv:dv + 1]))

    for i in range(len(items) + LOOKAHEAD):
        if i < len(items):
            issue_scores(i)
        if i >= LOOKAHEAD:
            consume(i - LOOKAHEAD)


def _attn_kernel(*refs, lam_init, has_cache):
    it = iter(refs)
    zero_ref = next(it)
    lam_ref, dnw_ref = next(it), next(it)
    qdt_ref, kd_ref, vdt_ref, qmt_ref, kc_ref, vmt_ref = (next(it) for _ in range(6))
    kds, vdts, kcs, vmts = [kd_ref], [vdt_ref], [kc_ref], [vmt_ref]
    if has_cache:
        kds.append(next(it)); vdts.append(next(it)); kcs.append(next(it)); vmts.append(next(it))
    yb_ref, yc_ref = next(it), next(it)
    s_ring = [next(it) for _ in range(LOOKAHEAD + 1)]
    p_ring = [next(it) for _ in range(2)]

    lv = lam_ref[0]
    lam = (jnp.exp(jnp.sum(lv[0:1] * lv[1:2], axis=-1, keepdims=True))
           - jnp.exp(jnp.sum(lv[2:3] * lv[3:4], axis=-1, keepdims=True)) + lam_init)

    def diff_rhs(hm):
        def fn():
            q_dt = qdt_ref[...]
            feat = lax.broadcasted_iota(jnp.int32, q_dt.shape, 0)
            return jnp.where((feat >= hm * DIFF_DK) & (feat < (hm + 1) * DIFF_DK), q_dt, jnp.zeros_like(q_dt))
        return fn

    def mla_rhs(feats):
        return lambda: qmt_ref[feats, :]

    heads = []
    for hm in range(2 * DIFF_HEADS):
        rows = slice((hm // 2) * DIFF_DV, (hm // 2 + 1) * DIFF_DV)
        heads.append(_Head(kds, diff_rhs(hm), [vt.at[rows, :] for vt in vdts]))
    for hd in range(MLA_HEADS):
        feats = slice(hd * MLA_HEAD_PAD, (hd + 1) * MLA_HEAD_PAD)
        rows = slice(hd * MLA_V, (hd + 1) * MLA_V)
        heads.append(_Head([kc.at[:, feats] for kc in kcs], mla_rhs(feats), [vt.at[rows, :] for vt in vmts]))

    outs = []

    def finish(t, o):
        outs.append(o)
        if t == 2 * DIFF_HEADS - 1:
            y_heads = []
            for hd in range(DIFF_HEADS):
                d = outs[2 * hd] - lam * outs[2 * hd + 1]
                ms = jnp.mean(d * d, axis=0, keepdims=True)
                y_heads.append(d * lax.rsqrt(ms + 1e-6))
            y_b = jnp.concatenate(y_heads, axis=0).T
            yb_ref[...] = (y_b * dnw_ref[0] * (1.0 - lam_init)).astype(BF16)
        if t == len(heads) - 1:
            yc_ref[...] = jnp.concatenate(outs[2 * DIFF_HEADS:], axis=0).T.astype(BF16)

    _run_heads(heads, finish, zero_ref, s_ring, p_ring)


def _attn_call(lamv, dnw, qdt, kd, vdt, qmt, kc, vmt, cache, l, *, n_batch, seq, tq, lam_init):
    t_per_seq = seq // tq
    has_cache = cache is not None

    def q_spec(width):
        return pl.BlockSpec((None, width, tq), lambda b, t, z: (b, 0, t))

    def out_spec(width):
        return pl.BlockSpec((tq, width), lambda b, t, z: (b * t_per_seq + t, 0))

    in_specs = [
        pl.BlockSpec((1, 4, DIFF_DK), lambda b, t, z: (l, 0, 0)),
        pl.BlockSpec((1, 1, DIFF_WIDTH), lambda b, t, z: (l, 0, 0)),
        q_spec(DIFF_QK_COLS),
        pl.BlockSpec((seq, DIFF_QK_COLS), lambda b, t, z: (b, 0)),
        pl.BlockSpec((None, DIFF_WIDTH, seq), lambda b, t, z: (b, 0, 0)),
        q_spec(MLA_QK_PAD),
        pl.BlockSpec((seq, MLA_QK_PAD), lambda b, t, z: (b, 0)),
        pl.BlockSpec((None, MLA_WIDTH, seq), lambda b, t, z: (b, 0, 0)),
    ]
    args = [lamv, dnw, qdt, kd, vdt, qmt, kc, vmt]
    past = 0
    if has_cache:
        past = cache[0].shape[2]
        in_specs += [
            pl.BlockSpec((None, None, past, DIFF_QK_COLS), lambda b, t, z: (b, l, 0, 0)),
            pl.BlockSpec((None, None, DIFF_WIDTH, past), lambda b, t, z: (b, l, 0, 0)),
            pl.BlockSpec((None, None, past, MLA_QK_PAD), lambda b, t, z: (b, l, 0, 0)),
            pl.BlockSpec((None, None, MLA_WIDTH, past), lambda b, t, z: (b, l, 0, 0)),
        ]
        args += list(cache)
    rows = n_batch * seq
    ring = (2 * MXU_DIM, tq)
    return pl.pallas_call(
        functools.partial(_attn_kernel, lam_init=lam_init, has_cache=has_cache),
        out_shape=(jax.ShapeDtypeStruct((rows, DIFF_WIDTH), BF16),
                   jax.ShapeDtypeStruct((rows, MLA_WIDTH), BF16)),
        grid_spec=pltpu.PrefetchScalarGridSpec(
            num_scalar_prefetch=1,
            grid=(n_batch, t_per_seq),
            in_specs=in_specs,
            out_specs=(out_spec(DIFF_WIDTH), out_spec(MLA_WIDTH)),
            scratch_shapes=[pltpu.VMEM(ring, F32)] * (LOOKAHEAD + 1) + [pltpu.VMEM(ring, BF16)] * 2),
        compiler_params=_cparams(2), name=f"attn_l{l}_{'s' if has_cache else 'p'}",
    )(jnp.zeros((2,), jnp.int32), *args)


def _tail_kernel(x_ref, u_ref, up_ref, un_ref, ab_ref, yb_ref, yc_ref, cw_ref, g1_ref, sh2_ref, sc2_ref, g2_ref,
                 ln1g_ref, ln1b_ref, ln2g_ref, ln2b_ref, w_out_ref, w1_ref, w3_ref, w2_ref, o_ref, *, seq):
    tm = x_ref.shape[0]
    u = u_ref[...]
    row = lax.broadcasted_iota(jnp.int32, (tm, 1), 0)
    pos = (row + pl.program_id(0) * tm) % seq
    u_prev = jnp.where(row == 0, up_ref[SUBLANES - 1:SUBLANES, :], pltpu.roll(u, 1, axis=0))
    u_prev = jnp.where(pos == 0, 0.0, u_prev)
    u_next = jnp.where(row == tm - 1, un_ref[0:1, :], pltpu.roll(u, tm - 1, axis=0))
    u_next = jnp.where(pos == seq - 1, 0.0, u_next)
    cw = cw_ref[0]
    y_a = (ab_ref[...] * (u_prev * cw[0:1] + u * cw[1:2] + u_next * cw[2:3])).astype(BF16)

    n_sub = tm // TAIL_SUB
    st = [dict(rows=slice(r * TAIL_SUB, (r + 1) * TAIL_SUB)) for r in range(n_sub)]

    def out_proj(d):
        y_cat = jnp.concatenate([y_a[d["rows"]], yb_ref[d["rows"], :], yc_ref[d["rows"], :]], axis=1)
        d["y"] = jnp.dot(y_cat, w_out_ref[...], preferred_element_type=F32)

    def norm1(d):
        d["x1"] = _layer_norm(DEEPNORM_ALPHA * x_ref[d["rows"], :] + g1_ref[0] * d.pop("y"),
                              ln1g_ref[0], ln1b_ref[0])
        d["hf"] = (d["x1"] * (1.0 + sc2_ref[0]) + sh2_ref[0]).astype(BF16)
        d["f"] = None

    def ffn_chunk(c0, c1):
        def stage(d):
            g = jnp.dot(d["hf"], w1_ref[:, c0:c1], preferred_element_type=F32)
            a = (g * _sigmoid(g) * jnp.dot(d["hf"], w3_ref[:, c0:c1], preferred_element_type=F32)).astype(BF16)
            t = jnp.dot(a, w2_ref[c0:c1, :], preferred_element_type=F32)
            d["f"] = t if d["f"] is None else d["f"] + t
        return stage

    def norm2(d):
        o_ref[d["rows"], :] = _layer_norm(DEEPNORM_ALPHA * d.pop("x1") + g2_ref[0] * d.pop("f"),
                                          ln2g_ref[0], ln2b_ref[0])

    stages = [out_proj, norm1] + [ffn_chunk(c0, c1) for c0, c1 in FF_CHUNKS] + [norm2]
    for step in range(len(stages) + n_sub - 1):
        for r in range(n_sub):
            if 0 <= step - r < len(stages):
                stages[step - r](st[r])


def _tail_call(x2d, u, ab, yb, yc, mod3, wts, l, *, seq, tm, mod_row):
    rows = x2d.shape[0]
    n_tiles = rows // tm
    n_halo = rows // SUBLANES

    def mod_spec(j):
        return pl.BlockSpec((1, 1, D_MODEL), lambda i: ((l * COND_ROWS + mod_row(i)) * N_MOD + j, 0, 0))

    def row_spec(width):
        return pl.BlockSpec((tm, width), lambda i: (i, 0))

    in_specs = [
        row_spec(D_MODEL), row_spec(CONV_W),
        pl.BlockSpec((SUBLANES, CONV_W), lambda i: (jnp.maximum(i * (tm // SUBLANES) - 1, 0), 0)),
        pl.BlockSpec((SUBLANES, CONV_W), lambda i: (jnp.minimum((i + 1) * (tm // SUBLANES), n_halo - 1), 0)),
        row_spec(CONV_W), row_spec(DIFF_WIDTH), row_spec(MLA_WIDTH),
        pl.BlockSpec((1, 3, CONV_W), lambda i: (l, 0, 0)),
        mod_spec(2), mod_spec(3), mod_spec(4), mod_spec(5),
        _layer_spec((1, D_MODEL), l), _layer_spec((1, D_MODEL), l),
        _layer_spec((1, D_MODEL), l), _layer_spec((1, D_MODEL), l),
        _layer_spec((D_MODEL, D_MODEL), l),
        _layer_spec((D_MODEL, D_FF), l), _layer_spec((D_MODEL, D_FF), l), _layer_spec((D_FF, D_MODEL), l),
    ]
    return pl.pallas_call(
        functools.partial(_tail_kernel, seq=seq),
        out_shape=jax.ShapeDtypeStruct((rows, D_MODEL), F32),
        grid=(n_tiles,),
        in_specs=in_specs,
        out_specs=row_spec(D_MODEL),
        compiler_params=_cparams(1), name=f"tail_l{l}",
    )(x2d, u, u, u, ab, yb, yc, wts["conv_w"], mod3, mod3, mod3, mod3,
      wts["ln1_g"], wts["ln1_b"], wts["ln2_g"], wts["ln2_b"],
      wts["w_out"], wts["w_ff1"], wts["w_ff3"], wts["w_ff2"])


def _rope_tables(n_tokens):
    rows = n_tokens // GRID_W
    row = np.repeat(np.arange(rows), GRID_W).astype(np.float32)
    col = np.tile(np.arange(GRID_W), rows).astype(np.float32)
    half = MLA_ROPE // 4
    freqs = (np.float32(ROPE_BASE) ** (-np.arange(half, dtype=np.float32) / np.float32(half))).astype(np.float32)
    ang_r = row[:, None] * freqs[None, :]
    ang_c = col[:, None] * freqs[None, :]
    cos32 = np.concatenate([np.cos(ang_r), np.cos(ang_r), np.cos(ang_c), np.cos(ang_c)], axis=1)
    sin32 = np.concatenate([-np.sin(ang_r), np.sin(ang_r), -np.sin(ang_c), np.sin(ang_c)], axis=1)
    reps = LANES // cos32.shape[1]
    return tuple(jnp.asarray(np.tile(t, (1, reps)).astype(np.float32)) for t in (cos32, sin32))


def _prep_weights(w_in, w_uq, w_ukv, w_out, w_ff1, w_ff3, w_ff2, conv_w, q_norm_w, kv_norm_w,
                  ln1_g, ln1_b, ln2_g, ln2_b):
    n_l = w_in.shape[0]
    w_in_p = jnp.concatenate(
        [w_in.astype(BF16), jnp.zeros((n_l, D_MODEL, IN_COLS_PAD - IN_COLS), BF16)], axis=-1)
    w_uq_p = jnp.pad(w_uq.reshape(n_l, MLA_Q_RANK, MLA_HEADS, MLA_QK),
                     ((0, 0), (0, 0), (0, 0), (0, MLA_HEAD_PAD - MLA_QK))).reshape(n_l, MLA_Q_RANK, MLA_QK_PAD)
    w_ukv4 = w_ukv.reshape(n_l, MLA_KV_RANK, MLA_HEADS, MLA_NOPE + MLA_V)
    w_uk_p = jnp.pad(w_ukv4[..., :MLA_NOPE], ((0, 0), (0, 0), (0, 0), (0, MLA_HEAD_PAD - MLA_NOPE)))
    w_kv = jnp.concatenate([w_uk_p.reshape(n_l, MLA_KV_RANK, MLA_QK_PAD),
                            w_ukv4[..., MLA_NOPE:].reshape(n_l, MLA_KV_RANK, MLA_WIDTH)], axis=-1)
    vec = lambda a: a.reshape(n_l, 1, a.shape[-1])
    return {
        "w_in": w_in_p, "w_uq": w_uq_p.astype(BF16), "w_kv": w_kv.astype(BF16),
        "w_out": w_out.astype(BF16), "w_ff1": w_ff1.astype(BF16), "w_ff3": w_ff3.astype(BF16),
        "w_ff2": w_ff2.astype(BF16), "conv_w": conv_w,
        "q_norm_w": vec(q_norm_w), "kv_norm_w": vec(kv_norm_w),
        "ln1_g": vec(ln1_g), "ln1_b": vec(ln1_b), "ln2_g": vec(ln2_g), "ln2_b": vec(ln2_b),
    }


def kernel(x_prompt, x_sample, cache_diff_k, cache_diff_v, cache_mla_ckv, cache_mla_kpe, c, c_ctx,
           w_ada, b_ada, w_in, conv_w, lam_q1, lam_k1, lam_q2, lam_k2, diff_norm_w, q_norm_w, w_uq,
           kv_norm_w, w_ukv, w_out, ln1_g, ln1_b, w_ff1, w_ff3, w_ff2, ln2_g, ln2_b):
    n_l = w_in.shape[0]
    pb, ps, _ = x_prompt.shape
    sb, ss, _ = x_sample.shape
    past = cache_mla_ckv.shape[2]

    wts = _prep_weights(w_in, w_uq, w_ukv, w_out, w_ff1, w_ff3, w_ff2, conv_w, q_norm_w, kv_norm_w,
                        ln1_g, ln1_b, ln2_g, ln2_b)
    lamv = jnp.stack([lam_q1, lam_k1, lam_q2, lam_k2], axis=1)
    dnw = jnp.tile(diff_norm_w, (1, DIFF_HEADS)).reshape(n_l, 1, DIFF_WIDTH)

    cond = jnp.concatenate([c_ctx[None, :], c, jnp.zeros((COND_ROWS - 1 - sb, D_MODEL), F32)], axis=0)
    mod3 = _ada_call(cond, w_ada, b_ada).reshape(n_l * COND_ROWS * N_MOD, 1, D_MODEL)

    ck = jnp.transpose(cache_diff_k, (0, 1, 4, 2, 3, 5)).reshape(sb, n_l, past, DIFF_QK_COLS).astype(BF16)
    cv_t = jnp.swapaxes(cache_diff_v, -1, -2).reshape(sb, n_l, DIFF_WIDTH, past).astype(BF16)
    kpe_slab = jnp.pad(cache_mla_kpe, ((0, 0), (0, 0), (0, 0), (0, LANES - MLA_ROPE)))
    ckc, cvm_t = _cache_kv_call(cache_mla_ckv, kpe_slab, wts["w_kv"])

    rope_tabs = _rope_tables(ss)
    tm_in_p, tm_in, tm_tail, tq_s = 512, 1024, 512, 512

    def sample_row(tm):
        return lambda i: 1 + i // (ss // tm)

    xp = x_prompt.reshape(pb * ps, D_MODEL)
    xs = x_sample.reshape(sb * ss, D_MODEL)
    states = ()
    for l in range(n_l):
        lam_init = 0.8 - 0.6 * math.exp(-0.3 * l)

        row_p = lambda i: 0
        outs = _inproj_call(xp, mod3, wts, l, n_batch=pb, seq=ps, tm=tm_in_p, mod_row=row_p, rope_tabs=None,
                            n_layers=n_l, states=states)
        u, ab, qdt, kd, vdt, qmt, kc, vmt = outs[:8]
        states = tuple(outs[8:])
        yb, yc = _attn_call(lamv, dnw, qdt, kd, vdt, qmt, kc, vmt, None, l,
                            n_batch=pb, seq=ps, tq=ps, lam_init=lam_init)
        xp = _tail_call(xp, u, ab, yb, yc, mod3, wts, l, seq=ps, tm=tm_tail, mod_row=row_p)

        u, ab, qdt, kd, vdt, qmt, kc, vmt = _inproj_call(
            xs, mod3, wts, l, n_batch=sb, seq=ss, tm=tm_in, mod_row=sample_row(tm_in), rope_tabs=rope_tabs,
            n_layers=n_l, states=None)
        yb, yc = _attn_call(lamv, dnw, qdt, kd, vdt, qmt, kc, vmt, (ck, cv_t, ckc, cvm_t), l,
                            n_batch=sb, seq=ss, tq=tq_s, lam_init=lam_init)
        xs = _tail_call(xs, u, ab, yb, yc, mod3, wts, l, seq=ss, tm=tm_tail, mod_row=sample_row(tm_tail))

    return (xp.reshape(pb, ps, D_MODEL), xs.reshape(sb, ss, D_MODEL)) + states
```

```python
import functools
import math

import numpy as np
import jax
import jax.numpy as jnp
from jax import lax
from jax.experimental import pallas as pl
from jax.experimental.pallas import tpu as pltpu

F32 = jnp.float32
BF16 = jnp.bfloat16

D_MODEL = 1024
DEPTH = 2
GRID_W = 64
ROPE_BASE = 10000.0
CONV_W = 256
DIFF_HEADS = 4
DIFF_DK = 32
DIFF_DV = 64
DIFF_WIDTH = DIFF_HEADS * DIFF_DV
DIFF_QK_COLS = DIFF_HEADS * 2 * DIFF_DK
DIFF_SCALE = DIFF_DK ** -0.5
MLA_HEADS = 8
MLA_Q_RANK = 384
MLA_KV_RANK = 256
MLA_NOPE = 64
MLA_ROPE = 32
MLA_V = 64
MLA_QK = MLA_NOPE + MLA_ROPE
MLA_WIDTH = MLA_HEADS * MLA_V
MLA_SCALE = MLA_QK ** -0.5
IN_COLS = 3 * CONV_W + 2 * DIFF_QK_COLS + DIFF_WIDTH + MLA_Q_RANK + MLA_KV_RANK + MLA_ROPE
D_FF = 2816
DEEPNORM_ALPHA = (2 * DEPTH) ** 0.25
LOG2E = 1.4426950408889634

LANES = 128
SUBLANES = 8
BF16_ROWS = 16
MXU_DIM = 256
IN_COLS_PAD = 2304
MLA_HEAD_PAD = LANES
MLA_QK_PAD = MLA_HEADS * MLA_HEAD_PAD
N_MOD = 6
COND_ROWS = 8
FF_CHUNKS = ((0, 1536), (1536, 2816))
KEY_CHUNK = 64
LOOKAHEAD = 3
TAIL_SUB = 256
INPROJ_SUB = 256
VMEM_LIMIT = 56 * 2 ** 20

OFF_AX, OFF_AB, OFF_AC = 0, 256, 512
OFF_DQ, OFF_DK, OFF_DV = 768, 1024, 1280
OFF_CQ, OFF_CKV, OFF_KPE = 1536, 1920, 2176


def _cparams(n_grid):
    return pltpu.CompilerParams(dimension_semantics=("arbitrary",) * n_grid,
                                vmem_limit_bytes=VMEM_LIMIT)


def _layer_spec(tail, l):
    nd = len(tail)
    return pl.BlockSpec((None,) + tuple(tail), lambda *_: (l,) + (0,) * nd,
                        pipeline_mode=pl.Buffered(1))


def _sigmoid(x):
    return 1.0 / (1.0 + jnp.exp(-x))


def _layer_norm(x, g, b):
    mu = jnp.mean(x, axis=-1, keepdims=True)
    xc = x - mu
    var = jnp.mean(xc * xc, axis=-1, keepdims=True)
    return xc * lax.rsqrt(var + 1e-5) * g + b


def _rms_norm(x, w):
    ms = jnp.mean(x * x, axis=-1, keepdims=True)
    return x * lax.rsqrt(ms + 1e-6) * w


def _rope_slab(x, cos, sin_signed):
    lane = lax.broadcasted_iota(jnp.int32, x.shape, 1)
    lo = (lane % 16) < 8
    partner = jnp.where(lo, pltpu.roll(x, LANES - 8, axis=1), pltpu.roll(x, 8, axis=1))
    return x * cos + partner * sin_signed


def _rope_window(cos, sin_signed, lane0):
    lane = lax.broadcasted_iota(jnp.int32, cos.shape, 1)
    inside = (lane >= lane0) & (lane < lane0 + MLA_ROPE)
    return jnp.where(inside, cos, 1.0), jnp.where(inside, sin_signed, 0.0)


def _ada_kernel(cond_ref, w_ref, b_ref, o_ref):
    c = cond_ref[...]
    s = (c * _sigmoid(c)).astype(BF16)
    o_ref[0] = jnp.dot(s, w_ref[0].astype(BF16), preferred_element_type=F32) + b_ref[0]


def _ada_call(cond, w_ada, b_ada):
    n_l = w_ada.shape[0]
    tn = 2 * D_MODEL
    return pl.pallas_call(
        _ada_kernel,
        out_shape=jax.ShapeDtypeStruct((n_l, COND_ROWS, N_MOD * D_MODEL), F32),
        grid=(n_l, N_MOD * D_MODEL // tn),
        in_specs=[
            pl.BlockSpec((COND_ROWS, D_MODEL), lambda l, j: (0, 0)),
            pl.BlockSpec((1, D_MODEL, tn), lambda l, j: (l, 0, j)),
            pl.BlockSpec((1, 1, tn), lambda l, j: (l, 0, j)),
        ],
        out_specs=pl.BlockSpec((1, COND_ROWS, tn), lambda l, j: (l, 0, j)),
        compiler_params=_cparams(2),
        name="ada_mod",
    )(cond, w_ada, b_ada.reshape(n_l, 1, N_MOD * D_MODEL))


def _expand_kv(ckv, kpe_slab, w_kv_ref):
    kv = jnp.dot(ckv.astype(BF16), w_kv_ref[...], preferred_element_type=F32)
    kpe_shift = pltpu.roll(kpe_slab, MLA_NOPE, axis=1)
    k_cat = jnp.concatenate(
        [(kv[:, h * MLA_HEAD_PAD:(h + 1) * MLA_HEAD_PAD] + kpe_shift).astype(BF16) for h in range(MLA_HEADS)],
        axis=1)
    return k_cat, kv[:, MLA_QK_PAD:MLA_QK_PAD + MLA_WIDTH]


def _cache_kv_kernel(ckv_ref, kpe_ref, w_kv_ref, kc_ref, vmt_ref):
    k_cat, v_m = _expand_kv(ckv_ref[...], kpe_ref[...], w_kv_ref)
    kc_ref[...] = k_cat
    vmt_ref[...] = v_m.T.astype(BF16)


def _cache_kv_call(cache_ckv, cache_kpe_slab, w_kv):
    n_b, n_l, past, _ = cache_ckv.shape
    return pl.pallas_call(
        _cache_kv_kernel,
        out_shape=(jax.ShapeDtypeStruct((n_b, n_l, past, MLA_QK_PAD), BF16),
                   jax.ShapeDtypeStruct((n_b, n_l, MLA_WIDTH, past), BF16)),
        grid=(n_b, n_l),
        in_specs=[
            pl.BlockSpec((None, None, past, MLA_KV_RANK), lambda b, l: (b, l, 0, 0)),
            pl.BlockSpec((None, None, past, LANES), lambda b, l: (b, l, 0, 0)),
            pl.BlockSpec((None, MLA_KV_RANK, MLA_QK_PAD + MLA_WIDTH), lambda b, l: (l, 0, 0)),
        ],
        out_specs=(pl.BlockSpec((None, None, past, MLA_QK_PAD), lambda b, l: (b, l, 0, 0)),
                   pl.BlockSpec((None, None, MLA_WIDTH, past), lambda b, l: (b, l, 0, 0))),
        compiler_params=_cparams(2),
        name="cache_kv",
    )(cache_ckv, cache_kpe_slab, w_kv)


N_STATE = 4


def _inproj_kernel(*refs, rope, n_seq, seq, with_state, alias_state):
    it = iter(refs)
    x_ref, sc_ref, sh_ref = next(it), next(it), next(it)
    w_in_ref, qnw_ref, w_uq_ref, kvnw_ref, w_kv_ref = next(it), next(it), next(it), next(it), next(it)
    if rope:
        cd_ref, sd_ref = next(it), next(it)
    if alias_state:
        for _ in range(N_STATE):
            next(it)
    u_ref, ab_ref, qdt_ref, kd_ref, vdt_ref, qmt_ref, kc_ref, vmt_ref = (next(it) for _ in range(8))
    if with_state:
        stk_ref, stv_ref, stc_ref, stp_ref = (next(it) for _ in range(N_STATE))

    tm = x_ref.shape[0]
    sub = seq if n_seq else INPROJ_SUB
    st = [dict(r=r, rows=slice(r * sub, (r + 1) * sub)) for r in range(tm // sub)]

    def project(d):
        h = (x_ref[d["rows"], :] * (1.0 + sc_ref[0]) + sh_ref[0]).astype(BF16)
        d["proj"] = jnp.dot(h, w_in_ref[...], preferred_element_type=F32)

    def split(d):
        proj, rows = d.pop("proj"), d["rows"]
        u_ref[rows, :] = proj[:, OFF_AC:OFF_AC + CONV_W] * proj[:, OFF_AX:OFF_AX + CONV_W]
        ab_ref[rows, :] = proj[:, OFF_AB:OFF_AB + CONV_W]
        q = proj[:, OFF_DQ:OFF_DQ + DIFF_QK_COLS]
        k = proj[:, OFF_DK:OFF_DK + DIFF_QK_COLS]
        d["v"] = proj[:, OFF_DV:OFF_DV + DIFF_WIDTH]
        if rope:
            cd, sd = cd_ref[rows, :], sd_ref[rows, :]
            q = jnp.concatenate([_rope_slab(q[:, j * LANES:(j + 1) * LANES], cd, sd) for j in range(2)], axis=1)
            k = jnp.concatenate([_rope_slab(k[:, j * LANES:(j + 1) * LANES], cd, sd) for j in range(2)], axis=1)
        kd_ref[rows, :] = k.astype(BF16)
        d["q"], d["k"] = q, k
        d["cq"] = _rms_norm(proj[:, OFF_CQ:OFF_CQ + MLA_Q_RANK], qnw_ref[0]).astype(BF16)
        d["ckv"] = _rms_norm(proj[:, OFF_CKV:OFF_CKV + MLA_KV_RANK], kvnw_ref[0])
        kpe_slab = proj[:, OFF_KPE:OFF_KPE + LANES]
        if rope:
            kpe_slab = _rope_slab(kpe_slab, *_rope_window(cd_ref[rows, :], sd_ref[rows, :], 0))
        d["kpe"] = kpe_slab

    def up_project(d):
        d["qc"] = jnp.dot(d.pop("cq"), w_uq_ref[...], preferred_element_type=F32)
        d["k_cat"], d["v_m"] = _expand_kv(d["ckv"], d["kpe"], w_kv_ref)

    def emit(d):
        r, rows = d["r"], d["rows"]
        qc = d.pop("qc")
        if rope:
            cm, sm = _rope_window(cd_ref[rows, :], sd_ref[rows, :], MLA_NOPE)
            qc = jnp.concatenate(
                [_rope_slab(qc[:, hd * MLA_HEAD_PAD:(hd + 1) * MLA_HEAD_PAD], cm, sm) for hd in range(MLA_HEADS)],
                axis=1)
        kc_ref[rows, :] = d.pop("k_cat")
        q_d, q_m = d["q"] * (DIFF_SCALE * LOG2E), qc * (MLA_SCALE * LOG2E)
        for dst, val in ((qdt_ref, q_d), (vdt_ref, d["v"]), (qmt_ref, q_m), (vmt_ref, d.pop("v_m"))):
            val_t = val.T.astype(BF16)
            if n_seq:
                dst[r] = val_t
            else:
                dst[:, rows] = val_t
        if with_state:
            fills = [()] if alias_state else [(ll,) for ll in range(stc_ref.shape[1])]
            for ll in fills:
                stc_ref[(r,) + ll] = d["ckv"]
                stp_ref[(r,) + ll] = d["kpe"][:, :MLA_ROPE]
                for hm in range(2 * DIFF_HEADS):
                    stk_ref[(r,) + ll + (hm // 2, hm % 2)] = d["k"][:, hm * DIFF_DK:(hm + 1) * DIFF_DK]
                for hd in range(DIFF_HEADS):
                    stv_ref[(r,) + ll + (hd,)] = d["v"][:, hd * DIFF_DV:(hd + 1) * DIFF_DV]

    stages = [project, split, up_project, emit]
    for step in range(len(stages) + len(st) - 1):
        for r, d in enumerate(st):
            if 0 <= step - r < len(stages):
                stages[step - r](d)


def _inproj_call(x2d, mod3, wts, l, *, n_batch, seq, tm, mod_row, rope_tabs, n_layers, states):
    rows = n_batch * seq
    n_tiles = rows // tm
    rope = rope_tabs is not None
    with_state = states is not None
    alias_state = bool(states)
    n_seq = tm // seq
    t_per_seq = max(seq // tm, 1)

    def mod_spec(j):
        return pl.BlockSpec((1, 1, D_MODEL), lambda i: ((l * COND_ROWS + mod_row(i)) * N_MOD + j, 0, 0))

    def row_spec(width):
        return pl.BlockSpec((tm, width), lambda i: (i, 0))

    def feat_spec(width):
        if n_seq:
            return pl.BlockSpec((n_seq, width, seq), lambda i: (i, 0, 0))
        return pl.BlockSpec((None, width, tm), lambda i: (i // t_per_seq, 0, i % t_per_seq))

    in_specs = [
        row_spec(D_MODEL), mod_spec(1), mod_spec(0),
        _layer_spec((D_MODEL, IN_COLS_PAD), l),
        _layer_spec((1, MLA_Q_RANK), l),
        _layer_spec((MLA_Q_RANK, MLA_QK_PAD), l),
        _layer_spec((1, MLA_KV_RANK), l),
        _layer_spec((MLA_KV_RANK, MLA_QK_PAD + MLA_WIDTH), l),
    ]
    args = [x2d, mod3, mod3, wts["w_in"], wts["q_norm_w"], wts["w_uq"], wts["kv_norm_w"], wts["w_kv"]]
    if rope:
        in_specs += [pl.BlockSpec((tm, LANES), lambda i: (i % t_per_seq, 0))] * 2
        args += list(rope_tabs)

    out_shape = [
        jax.ShapeDtypeStruct((rows, CONV_W), F32),
        jax.ShapeDtypeStruct((rows, CONV_W), F32),
        jax.ShapeDtypeStruct((n_batch, DIFF_QK_COLS, seq), BF16),
        jax.ShapeDtypeStruct((rows, DIFF_QK_COLS), BF16),
        jax.ShapeDtypeStruct((n_batch, DIFF_WIDTH, seq), BF16),
        jax.ShapeDtypeStruct((n_batch, MLA_QK_PAD, seq), BF16),
        jax.ShapeDtypeStruct((rows, MLA_QK_PAD), BF16),
        jax.ShapeDtypeStruct((n_batch, MLA_WIDTH, seq), BF16),
    ]
    out_specs = [row_spec(CONV_W), row_spec(CONV_W), feat_spec(DIFF_QK_COLS), row_spec(DIFF_QK_COLS),
                 feat_spec(DIFF_WIDTH), feat_spec(MLA_QK_PAD), row_spec(MLA_QK_PAD), feat_spec(MLA_WIDTH)]
    aliases = {}
    if with_state:
        assert n_seq >= 1
        n_out = len(out_shape)
        out_shape += [
            jax.ShapeDtypeStruct((n_batch, n_layers, DIFF_HEADS, 2, seq, DIFF_DK), F32),
            jax.ShapeDtypeStruct((n_batch, n_layers, DIFF_HEADS, seq, DIFF_DV), F32),
            jax.ShapeDtypeStruct((n_batch, n_layers, seq, MLA_KV_RANK), F32),
            jax.ShapeDtypeStruct((n_batch, n_layers, seq, MLA_ROPE), F32),
        ]
        lyr, l_idx = (None, l) if alias_state else (n_layers, 0)
        out_specs += [
            pl.BlockSpec((n_seq, lyr, DIFF_HEADS, 2, seq, DIFF_DK), lambda i: (i, l_idx, 0, 0, 0, 0)),
            pl.BlockSpec((n_seq, lyr, DIFF_HEADS, seq, DIFF_DV), lambda i: (i, l_idx, 0, 0, 0)),
            pl.BlockSpec((n_seq, lyr, seq, MLA_KV_RANK), lambda i: (i, l_idx, 0, 0)),
            pl.BlockSpec((n_seq, lyr, seq, MLA_ROPE), lambda i: (i, l_idx, 0, 0)),
        ]
        if alias_state:
            for j, st in enumerate(states):
                aliases[len(args)] = n_out + j
                in_specs.append(pl.BlockSpec(memory_space=pl.ANY))
                args.append(st)

    kern = functools.partial(_inproj_kernel, rope=rope, n_seq=n_seq, seq=seq, with_state=with_state,
                             alias_state=alias_state)
    return pl.pallas_call(
        kern, out_shape=tuple(out_shape), grid=(n_tiles,),
        in_specs=in_specs, out_specs=tuple(out_specs), input_output_aliases=aliases,
        compiler_params=_cparams(1), name=f"inproj_l{l}_{'s' if rope else 'p'}",
    )(*args)


def _tree(op, xs):
    xs = list(xs)
    while len(xs) > 1:
        xs = [op(xs[i], xs[i + 1]) if i + 1 < len(xs) else xs[i] for i in range(0, len(xs), 2)]
    return xs[0]


def _slabs(x):
    return [x[j * SUBLANES:(j + 1) * SUBLANES] for j in range(x.shape[0] // SUBLANES)]


class _Head:
    def __init__(self, k_refs, rhs_fn, vt_refs):
        self.k_blocks = [(k, r0) for k in k_refs for r0 in range(0, k.shape[0], MXU_DIM)]
        self.v_blocks = [(v, c0) for v in vt_refs for c0 in range(0, v.shape[1], MXU_DIM)]
        self.rhs_fn = rhs_fn


def _run_heads(heads, finish, zero_ref, s_ring, p_ring):
    items = [(h, j) for h in range(len(heads)) for j in range(len(heads[h].k_blocks))]
    z_st = pl.multiple_of(zero_ref[0], MXU_DIM)
    z_ld = pl.multiple_of(zero_ref[1], MXU_DIM)
    rhs = {}
    blk_max = {}
    state = {}

    def issue_scores(i):
        h, j = items[i]
        if j == 0:
            rhs[h] = heads[h].rhs_fn()
        k_ref, r0 = heads[h].k_blocks[j]
        s = jnp.dot(k_ref[r0:r0 + MXU_DIM, :], rhs[h], preferred_element_type=F32)
        s_ring[i % len(s_ring)][pl.ds(z_st, MXU_DIM), :] = s
        blk_max[i] = _tree(jnp.maximum, _slabs(s))

    def consume(i):
        h, j = items[i]
        head = heads[h]
        s_buf, p_buf = s_ring[i % len(s_ring)], p_ring[i % len(p_ring)]
        m_blk = jnp.max(blk_max.pop(i), axis=0, keepdims=True)
        m_new = m_blk if j == 0 else jnp.maximum(state["m"], m_blk)
        for c0 in range(0, MXU_DIM, KEY_CHUNK):
            p_buf[pl.ds(z_st + c0, KEY_CHUNK), :] = jnp.exp2(
                (s_buf[pl.ds(z_ld + c0, KEY_CHUNK), :] - m_new).astype(BF16))
        v_ref, c0 = head.v_blocks[j]
        v_ext = jnp.concatenate([v_ref[:, c0:c0 + MXU_DIM], jnp.ones((BF16_ROWS, MXU_DIM), BF16)], axis=0)
        pv = jnp.dot(v_ext, p_buf[pl.ds(z_ld, MXU_DIM), :], preferred_element_type=F32)
        if j == 0:
            state["acc"] = pv
        else:
            alpha = jnp.exp2(state["m"] - m_new)
            state["acc"] = state["acc"] * alpha + pv
        state["m"] = m_new
        if j == len(head.k_blocks) - 1:
            acc = state["acc"]
            dv = acc.shape[0] - BF16_ROWS
            finish(h, acc[:dv] * (1.0 / acc[dv:dv + 1]))

    for i in range(len(items) + LOOKAHEAD):
        if i < len(items):
            issue_scores(i)
        if i >= LOOKAHEAD:
            consume(i - LOOKAHEAD)


def _attn_kernel(*refs, lam_init, has_cache):
    it = iter(refs)
    zero_ref = next(it)
    lam_ref, dnw_ref = next(it), next(it)
    qdt_ref, kd_ref, vdt_ref, qmt_ref, kc_ref, vmt_ref = (next(it) for _ in range(6))
    kds, vdts, kcs, vmts = [kd_ref], [vdt_ref], [kc_ref], [vmt_ref]
    if has_cache:
        kds.append(next(it)); vdts.append(next(it)); kcs.append(next(it)); vmts.append(next(it))
    yb_ref, yc_ref = next(it), next(it)
    s_ring = [next(it) for _ in range(LOOKAHEAD + 1)]
    p_ring = [next(it) for _ in range(2)]

    lv = lam_ref[0]
    lam = (jnp.exp(jnp.sum(lv[0:1] * lv[1:2], axis=-1, keepdims=True))
           - jnp.exp(jnp.sum(lv[2:3] * lv[3:4], axis=-1, keepdims=True)) + lam_init)

    def diff_rhs(hm):
        def fn():
            q_dt = qdt_ref[...]
            feat = lax.broadcasted_iota(jnp.int32, q_dt.shape, 0)
            return jnp.where((feat >= hm * DIFF_DK) & (feat < (hm + 1) * DIFF_DK), q_dt, jnp.zeros_like(q_dt))
        return fn

    def mla_rhs(feats):
        return lambda: qmt_ref[feats, :]

    heads = []
    for hm in range(2 * DIFF_HEADS):
        rows = slice((hm // 2) * DIFF_DV, (hm // 2 + 1) * DIFF_DV)
        heads.append(_Head(kds, diff_rhs(hm), [vt.at[rows, :] for vt in vdts]))
    for hd in range(MLA_HEADS):
        feats = slice(hd * MLA_HEAD_PAD, (hd + 1) * MLA_HEAD_PAD)
        rows = slice(hd * MLA_V, (hd + 1) * MLA_V)
        heads.append(_Head([kc.at[:, feats] for kc in kcs], mla_rhs(feats), [vt.at[rows, :] for vt in vmts]))

    outs = []

    def finish(t, o):
        outs.append(o)
        if t == 2 * DIFF_HEADS - 1:
            y_heads = []
            for hd in range(DIFF_HEADS):
                d = outs[2 * hd] - lam * outs[2 * hd + 1]
                ms = jnp.mean(d * d, axis=0, keepdims=True)
                y_heads.append(d * lax.rsqrt(ms + 1e-6))
            y_b = jnp.concatenate(y_heads, axis=0).T
            yb_ref[...] = (y_b * dnw_ref[0] * (1.0 - lam_init)).astype(BF16)
        if t == len(heads) - 1:
            yc_ref[...] = jnp.concatenate(outs[2 * DIFF_HEADS:], axis=0).T.astype(BF16)

    _run_heads(heads, finish, zero_ref, s_ring, p_ring)


def _attn_call(lamv, dnw, qdt, kd, vdt, qmt, kc, vmt, cache, l, *, n_batch, seq, tq, lam_init):
    t_per_seq = seq // tq
    has_cache = cache is not None

    def q_spec(width):
        return pl.BlockSpec((None, width, tq), lambda b, t, z: (b, 0, t))

    def out_spec(width):
        return pl.BlockSpec((tq, width), lambda b, t, z: (b * t_per_seq + t, 0))

    in_specs = [
        pl.BlockSpec((1, 4, DIFF_DK), lambda b, t, z: (l, 0, 0)),
        pl.BlockSpec((1, 1, DIFF_WIDTH), lambda b, t, z: (l, 0, 0)),
        q_spec(DIFF_QK_COLS),
        pl.BlockSpec((seq, DIFF_QK_COLS), lambda b, t, z: (b, 0)),
        pl.BlockSpec((None, DIFF_WIDTH, seq), lambda b, t, z: (b, 0, 0)),
        q_spec(MLA_QK_PAD),
        pl.BlockSpec((seq, MLA_QK_PAD), lambda b, t, z: (b, 0)),
        pl.BlockSpec((None, MLA_WIDTH, seq), lambda b, t, z: (b, 0, 0)),
    ]
    args = [lamv, dnw, qdt, kd, vdt, qmt, kc, vmt]
    past = 0
    if has_cache:
        past = cache[0].shape[2]
        in_specs += [
            pl.BlockSpec((None, None, past, DIFF_QK_COLS), lambda b, t, z: (b, l, 0, 0)),
            pl.BlockSpec((None, None, DIFF_WIDTH, past), lambda b, t, z: (b, l, 0, 0)),
            pl.BlockSpec((None, None, past, MLA_QK_PAD), lambda b, t, z: (b, l, 0, 0)),
            pl.BlockSpec((None, None, MLA_WIDTH, past), lambda b, t, z: (b, l, 0, 0)),
        ]
        args += list(cache)
    rows = n_batch * seq
    ring = (2 * MXU_DIM, tq)
    return pl.pallas_call(
        functools.partial(_attn_kernel, lam_init=lam_init, has_cache=has_cache),
        out_shape=(jax.ShapeDtypeStruct((rows, DIFF_WIDTH), BF16),
                   jax.ShapeDtypeStruct((rows, MLA_WIDTH), BF16)),
        grid_spec=pltpu.PrefetchScalarGridSpec(
            num_scalar_prefetch=1,
            grid=(n_batch, t_per_seq),
            in_specs=in_specs,
            out_specs=(out_spec(DIFF_WIDTH), out_spec(MLA_WIDTH)),
            scratch_shapes=[pltpu.VMEM(ring, F32)] * (LOOKAHEAD + 1) + [pltpu.VMEM(ring, BF16)] * 2),
        compiler_params=_cparams(2), name=f"attn_l{l}_{'s' if has_cache else 'p'}",
    )(jnp.zeros((2,), jnp.int32), *args)


def _tail_kernel(x_ref, u_ref, up_ref, un_ref, ab_ref, yb_ref, yc_ref, cw_ref, g1_ref, sh2_ref, sc2_ref, g2_ref,
                 ln1g_ref, ln1b_ref, ln2g_ref, ln2b_ref, w_out_ref, w1_ref, w3_ref, w2_ref, o_ref, *, seq):
    tm = x_ref.shape[0]
    u = u_ref[...]
    row = lax.broadcasted_iota(jnp.int32, (tm, 1), 0)
    pos = (row + pl.program_id(0) * tm) % seq
    u_prev = jnp.where(row == 0, up_ref[SUBLANES - 1:SUBLANES, :], pltpu.roll(u, 1, axis=0))
    u_prev = jnp.where(pos == 0, 0.0, u_prev)
    u_next = jnp.where(row == tm - 1, un_ref[0:1, :], pltpu.roll(u, tm - 1, axis=0))
    u_next = jnp.where(pos == seq - 1, 0.0, u_next)
    cw = cw_ref[0]
    y_a = (ab_ref[...] * (u_prev * cw[0:1] + u * cw[1:2] + u_next * cw[2:3])).astype(BF16)

    n_sub = tm // TAIL_SUB
    st = [dict(rows=slice(r * TAIL_SUB, (r + 1) * TAIL_SUB)) for r in range(n_sub)]

    def out_proj(d):
        y_cat = jnp.concatenate([y_a[d["rows"]], yb_ref[d["rows"], :], yc_ref[d["rows"], :]], axis=1)
        d["y"] = jnp.dot(y_cat, w_out_ref[...], preferred_element_type=F32)

    def norm1(d):
        d["x1"] = _layer_norm(DEEPNORM_ALPHA * x_ref[d["rows"], :] + g1_ref[0] * d.pop("y"),
                              ln1g_ref[0], ln1b_ref[0])
        d["hf"] = (d["x1"] * (1.0 + sc2_ref[0]) + sh2_ref[0]).astype(BF16)
        d["f"] = None

    def ffn_chunk(c0, c1):
        def stage(d):
            g = jnp.dot(d["hf"], w1_ref[:, c0:c1], preferred_element_type=F32)
            a = (g * _sigmoid(g) * jnp.dot(d["hf"], w3_ref[:, c0:c1], preferred_element_type=F32)).astype(BF16)
            t = jnp.dot(a, w2_ref[c0:c1, :], preferred_element_type=F32)
            d["f"] = t if d["f"] is None else d["f"] + t
        return stage

    def norm2(d):
        o_ref[d["rows"], :] = _layer_norm(DEEPNORM_ALPHA * d.pop("x1") + g2_ref[0] * d.pop("f"),
                                          ln2g_ref[0], ln2b_ref[0])

    stages = [out_proj, norm1] + [ffn_chunk(c0, c1) for c0, c1 in FF_CHUNKS] + [norm2]
    for step in range(len(stages) + n_sub - 1):
        for r in range(n_sub):
            if 0 <= step - r < len(stages):
                stages[step - r](st[r])


def _tail_call(x2d, u, ab, yb, yc, mod3, wts, l, *, seq, tm, mod_row):
    rows = x2d.shape[0]
    n_tiles = rows // tm
    n_halo = rows // SUBLANES

    def mod_spec(j):
        return pl.BlockSpec((1, 1, D_MODEL), lambda i: ((l * COND_ROWS + mod_row(i)) * N_MOD + j, 0, 0))

    def row_spec(width):
        return pl.BlockSpec((tm, width), lambda i: (i, 0))

    in_specs = [
        row_spec(D_MODEL), row_spec(CONV_W),
        pl.BlockSpec((SUBLANES, CONV_W), lambda i: (jnp.maximum(i * (tm // SUBLANES) - 1, 0), 0)),
        pl.BlockSpec((SUBLANES, CONV_W), lambda i: (jnp.minimum((i + 1) * (tm // SUBLANES), n_halo - 1), 0)),
        row_spec(CONV_W), row_spec(DIFF_WIDTH), row_spec(MLA_WIDTH),
        pl.BlockSpec((1, 3, CONV_W), lambda i: (l, 0, 0)),
        mod_spec(2), mod_spec(3), mod_spec(4), mod_spec(5),
        _layer_spec((1, D_MODEL), l), _layer_spec((1, D_MODEL), l),
        _layer_spec((1, D_MODEL), l), _layer_spec((1, D_MODEL), l),
        _layer_spec((D_MODEL, D_MODEL), l),
        _layer_spec((D_MODEL, D_FF), l), _layer_spec((D_MODEL, D_FF), l), _layer_spec((D_FF, D_MODEL), l),
    ]
    return pl.pallas_call(
        functools.partial(_tail_kernel, seq=seq),
        out_shape=jax.ShapeDtypeStruct((rows, D_MODEL), F32),
        grid=(n_tiles,),
        in_specs=in_specs,
        out_specs=row_spec(D_MODEL),
        compiler_params=_cparams(1), name=f"tail_l{l}",
    )(x2d, u, u, u, ab, yb, yc, wts["conv_w"], mod3, mod3, mod3, mod3,
      wts["ln1_g"], wts["ln1_b"], wts["ln2_g"], wts["ln2_b"],
      wts["w_out"], wts["w_ff1"], wts["w_ff3"], wts["w_ff2"])


def _rope_tables(n_tokens):
    rows = n_tokens // GRID_W
    row = np.repeat(np.arange(rows), GRID_W).astype(np.float32)
    col = np.tile(np.arange(GRID_W), rows).astype(np.float32)
    half = MLA_ROPE // 4
    freqs = (np.float32(ROPE_BASE) ** (-np.arange(half, dtype=np.float32) / np.float32(half))).astype(np.float32)
    ang_r = row[:, None] * freqs[None, :]
    ang_c = col[:, None] * freqs[None, :]
    cos32 = np.concatenate([np.cos(ang_r), np.cos(ang_r), np.cos(ang_c), np.cos(ang_c)], axis=1)
    sin32 = np.concatenate([-np.sin(ang_r), np.sin(ang_r), -np.sin(ang_c), np.sin(ang_c)], axis=1)
    reps = LANES // cos32.shape[1]
    return tuple(jnp.asarray(np.tile(t, (1, reps)).astype(np.float32)) for t in (cos32, sin32))


def _prep_weights(w_in, w_uq, w_ukv, w_out, w_ff1, w_ff3, w_ff2, conv_w, q_norm_w, kv_norm_w,
                  ln1_g, ln1_b, ln2_g, ln2_b):
    n_l = w_in.shape[0]
    w_in_p = jnp.concatenate(
        [w_in.astype(BF16), jnp.zeros((n_l, D_MODEL, IN_COLS_PAD - IN_COLS), BF16)], axis=-1)
    w_uq_p = jnp.pad(w_uq.reshape(n_l, MLA_Q_RANK, MLA_HEADS, MLA_QK),
                     ((0, 0), (0, 0), (0, 0), (0, MLA_HEAD_PAD - MLA_QK))).reshape(n_l, MLA_Q_RANK, MLA_QK_PAD)
    w_ukv4 = w_ukv.reshape(n_l, MLA_KV_RANK, MLA_HEADS, MLA_NOPE + MLA_V)
    w_uk_p = jnp.pad(w_ukv4[..., :MLA_NOPE], ((0, 0), (0, 0), (0, 0), (0, MLA_HEAD_PAD - MLA_NOPE)))
    w_kv = jnp.concatenate([w_uk_p.reshape(n_l, MLA_KV_RANK, MLA_QK_PAD),
                            w_ukv4[..., MLA_NOPE:].reshape(n_l, MLA_KV_RANK, MLA_WIDTH)], axis=-1)
    vec = lambda a: a.reshape(n_l, 1, a.shape[-1])
    return {
        "w_in": w_in_p, "w_uq": w_uq_p.astype(BF16), "w_kv": w_kv.astype(BF16),
        "w_out": w_out.astype(BF16), "w_ff1": w_ff1.astype(BF16), "w_ff3": w_ff3.astype(BF16),
        "w_ff2": w_ff2.astype(BF16), "conv_w": conv_w,
        "q_norm_w": vec(q_norm_w), "kv_norm_w": vec(kv_norm_w),
        "ln1_g": vec(ln1_g), "ln1_b": vec(ln1_b), "ln2_g": vec(ln2_g), "ln2_b": vec(ln2_b),
    }


def kernel(x_prompt, x_sample, cache_diff_k, cache_diff_v, cache_mla_ckv, cache_mla_kpe, c, c_ctx,
           w_ada, b_ada, w_in, conv_w, lam_q1, lam_k1, lam_q2, lam_k2, diff_norm_w, q_norm_w, w_uq,
           kv_norm_w, w_ukv, w_out, ln1_g, ln1_b, w_ff1, w_ff3, w_ff2, ln2_g, ln2_b):
    n_l = w_in.shape[0]
    pb, ps, _ = x_prompt.shape
    sb, ss, _ = x_sample.shape
    past = cache_mla_ckv.shape[2]

    wts = _prep_weights(w_in, w_uq, w_ukv, w_out, w_ff1, w_ff3, w_ff2, conv_w, q_norm_w, kv_norm_w,
                        ln1_g, ln1_b, ln2_g, ln2_b)
    lamv = jnp.stack([lam_q1, lam_k1, lam_q2, lam_k2], axis=1)
    dnw = jnp.tile(diff_norm_w, (1, DIFF_HEADS)).reshape(n_l, 1, DIFF_WIDTH)

    cond = jnp.concatenate([c_ctx[None, :], c, jnp.zeros((COND_ROWS - 1 - sb, D_MODEL), F32)], axis=0)
    mod3 = _ada_call(cond, w_ada, b_ada).reshape(n_l * COND_ROWS * N_MOD, 1, D_MODEL)

    ck = jnp.transpose(cache_diff_k, (0, 1, 4, 2, 3, 5)).reshape(sb, n_l, past, DIFF_QK_COLS).astype(BF16)
    cv_t = jnp.swapaxes(cache_diff_v, -1, -2).reshape(sb, n_l, DIFF_WIDTH, past).astype(BF16)
    kpe_slab = jnp.pad(cache_mla_kpe, ((0, 0), (0, 0), (0, 0), (0, LANES - MLA_ROPE)))
    ckc, cvm_t = _cache_kv_call(cache_mla_ckv, kpe_slab, wts["w_kv"])

    rope_tabs = _rope_tables(ss)
    tm_in_p, tm_in, tm_tail, tq_s = 512, 1024, 512, 512

    def sample_row(tm):
        return lambda i: 1 + i // (ss // tm)

    xp = x_prompt.reshape(pb * ps, D_MODEL)
    xs = x_sample.reshape(sb * ss, D_MODEL)
    states = ()
    for l in range(n_l):
        lam_init = 0.8 - 0.6 * math.exp(-0.3 * l)

        row_p = lambda i: 0
        outs = _inproj_call(xp, mod3, wts, l, n_batch=pb, seq=ps, tm=tm_in_p, mod_row=row_p, rope_tabs=None,
                            n_layers=n_l, states=states)
        u, ab, qdt, kd, vdt, qmt, kc, vmt = outs[:8]
        states = tuple(outs[8:])
        yb, yc = _attn_call(lamv, dnw, qdt, kd, vdt, qmt, kc, vmt, None, l,
                            n_batch=pb, seq=ps, tq=ps, lam_init=lam_init)
        xp = _tail_call(xp, u, ab, yb, yc, mod3, wts, l, seq=ps, tm=tm_tail, mod_row=row_p)

        u, ab, qdt, kd, vdt, qmt, kc, vmt = _inproj_call(
            xs, mod3, wts, l, n_batch=sb, seq=ss, tm=tm_in, mod_row=sample_row(tm_in), rope_tabs=rope_tabs,
            n_layers=n_l, states=None)
        yb, yc = _attn_call(lamv, dnw, qdt, kd, vdt, qmt, kc, vmt, (ck, cv_t, ckc, cvm_t), l,
                            n_batch=sb, seq=ss, tq=tq_s, lam_init=lam_init)
        xs = _tail_call(xs, u, ab, yb, yc, mod3, wts, l, seq=ss, tm=tm_tail, mod_row=sample_row(tm_tail))

    return (xp.reshape(pb, ps, D_MODEL), xs.reshape(sb, ss, D_MODEL)) + states
```
